```python
import jax
import jax.numpy as jnp
from jax import lax
import numpy as np

D_MODEL = 2048
BATCH = 16
SEQ = 256
DEPTH = 4
DEC_BATCH = 4
DEC_SEQ = 1024
PAST_LEN = 256

GRID_W = 64
CONV_CH = D_MODEL // 4
CONV_WIDTH = 31
RET_HEADS = 4
RET_HD = D_MODEL // 16
RET_CHUNK = 128
ATT_HEADS = 8
KV_HEADS = 2
ATT_HD = D_MODEL // 16
ATT_GROUP = ATT_HEADS // KV_HEADS
ATT_WINDOW = 128
ATT_BLOCK = 128
ATT_SCALE = ATT_HD ** -0.5
ROPE_BASE = 10000.0
N_EXPERT_GROUPS = 4
EXPERTS_PER_GROUP = 8
N_EXPERTS = N_EXPERT_GROUPS * EXPERTS_PER_GROUP
EXPERT_TOPK = 2
D_EXPERT = D_MODEL // 4
MOE_BLOCK = 128
EPS = 1e-6
NEG_INF = -1e30
IN_SIZES = [CONV_CH, CONV_CH, RET_HEADS * RET_HD, RET_HEADS * RET_HD, RET_HEADS * RET_HD, RET_HEADS * RET_HD,
            ATT_HEADS * ATT_HD, KV_HEADS * ATT_HD, KV_HEADS * ATT_HD]
IN_WIDTH = sum(IN_SIZES)
SPLIT_POINTS = [sum(IN_SIZES[:i + 1]) for i in range(len(IN_SIZES) - 1)]

kernel_name = 'hybrid_dit_conv_retention_swa_hmoe'


def _rms_norm(x, g):
    xf = x.astype(jnp.float32)
    y = xf * lax.rsqrt(jnp.mean(xf * xf, axis=-1, keepdims=True) + EPS)
    return (y * g.astype(jnp.float32)).astype(x.dtype)


def _modulation(cvec, w, b):
    m = jax.nn.silu(cvec) @ w + b
    return [t[:, None, :] for t in jnp.split(m, 6, axis=-1)]


def _rope_1d(x, pos):
    nf = x.shape[-1] // 2
    inv = ROPE_BASE ** (-jnp.arange(nf, dtype=jnp.float32) / nf)
    ang = pos.astype(jnp.float32)[:, None] * inv[None, :]
    cos = jnp.cos(ang)[:, None, :]
    sin = jnp.sin(ang)[:, None, :]
    xf = x.astype(jnp.float32)
    x1, x2 = xf[..., :nf], xf[..., nf:]
    return jnp.concatenate([x1 * cos - x2 * sin, x1 * sin + x2 * cos], axis=-1).astype(x.dtype)


def _axial_rope(x):
    rows = x.shape[1] // GRID_W
    t = jnp.arange(rows * GRID_W)
    half = x.shape[-1] // 2
    return jnp.concatenate([_rope_1d(x[..., :half], t // GRID_W), _rope_1d(x[..., half:], t % GRID_W)], axis=-1)


def _conv_module(a, g, w, b, ln_g, ln_b):
    h = a * jax.nn.sigmoid(g)
    pad = CONV_WIDTH // 2
    h = lax.conv_general_dilated(h, w[:, None, :].astype(h.dtype), window_strides=(1,), padding=((pad, pad),),
                                 dimension_numbers=('NWC', 'WIO', 'NWC'), feature_group_count=CONV_CH)
    hf = h.astype(jnp.float32) + b.astype(jnp.float32)
    mu = jnp.mean(hf, axis=-1, keepdims=True)
    var = jnp.mean(jnp.square(hf - mu), axis=-1, keepdims=True)
    y = (hf - mu) * lax.rsqrt(var + EPS) * ln_g.astype(jnp.float32) + ln_b.astype(jnp.float32)
    return jax.nn.silu(y).astype(a.dtype)


def _retention_scan(q, k, v, log_gamma, s0, strict):
    B, N, H, Dh = q.shape
    C = RET_CHUNK
    n = N // C
    f32 = jnp.float32

    def to_chunks(t):
        return t.astype(f32).reshape(B, n, C, H, Dh).transpose(1, 0, 2, 3, 4)

    idx = jnp.arange(C, dtype=f32)
    rel = idx[:, None] - idx[None, :]
    mask = (rel > 0) if strict else (rel >= 0)
    intra = jnp.where(mask[None], jnp.exp(log_gamma[:, None, None] * jnp.maximum(rel, 0.0)[None]), 0.0)
    cross = jnp.exp(log_gamma[None, :] * (idx[:, None] + 1.0))
    inject = jnp.exp(log_gamma[:, None] * (C - 1.0 - idx)[None, :])
    chunk_decay = jnp.exp(log_gamma * C)

    def step(S, qkv):
        qc, kc, vc = qkv
        s = jnp.einsum('bchd,bshd->bhcs', qc, kc) * intra[None]
        o = jnp.einsum('bhcs,bshe->bche', s, vc) + jnp.einsum('bchd,bhde->bche', qc, S) * cross[None, :, :, None]
        S = S * chunk_decay[None, :, None, None] + jnp.einsum('bshd,bshe,hs->bhde', kc, vc, inject)
        return S, o

    S, o = lax.scan(step, s0.astype(f32), (to_chunks(q), to_chunks(k), to_chunks(v)))
    return o.transpose(1, 0, 2, 3, 4).reshape(B, N, H, Dh), S


def _retention(rq, rk, rv, rgate, decay_logits, s0_f, s0_b):
    B, N, _ = rq.shape
    q = rq.reshape(B, N, RET_HEADS, RET_HD)
    k = rk.reshape(B, N, RET_HEADS, RET_HD) * (RET_HD ** -0.5)
    v = rv.reshape(B, N, RET_HEADS, RET_HD)
    lg = jax.nn.log_sigmoid(decay_logits.astype(jnp.float32))
    o_f, s_f = _retention_scan(q, k, v, lg[0], s0_f, False)
    o_b, s_b = _retention_scan(q[:, ::-1], k[:, ::-1], v[:, ::-1], lg[1], s0_b, True)
    y = o_f + o_b[:, ::-1]
    y = y * lax.rsqrt(jnp.mean(y * y, axis=-1, keepdims=True) + EPS)
    out = jax.nn.silu(rgate.astype(jnp.float32)) * y.reshape(B, N, RET_HEADS * RET_HD)
    return out.astype(rq.dtype), s_f, s_b


def _sink_softmax(s, sink_l):
    logits = jnp.concatenate([s, jnp.broadcast_to(sink_l, s.shape[:-1] + (1,))], axis=-1)
    return jax.nn.softmax(logits, axis=-1)[..., :-1]


def _context_attention(q, k, v, sink):
    B, L, H, Dh = q.shape
    nb = L // ATT_BLOCK
    qb = q.reshape(B, nb, ATT_BLOCK, KV_HEADS, ATT_GROUP, Dh).transpose(1, 0, 2, 3, 4, 5)
    sink_l = sink.astype(jnp.float32).reshape(1, KV_HEADS, ATT_GROUP, 1, 1)

    def one(qi):
        s = jnp.einsum('bqkgd,bskd->bkgqs', qi, k).astype(jnp.float32) * ATT_SCALE
        p = _sink_softmax(s, sink_l)
        o = jnp.einsum('bkgqs,bskd->bqkgd', p.astype(v.dtype), v)
        return o.reshape(B, ATT_BLOCK, H, Dh)

    o = lax.map(one, qb)
    return o.transpose(1, 0, 2, 3, 4).reshape(B, L, H, Dh)


def _latent_attention(q, k, v, kc, vc, sink):
    B, N, H, Dh = q.shape
    T = ATT_BLOCK
    nb = N // T
    qb = q.reshape(B, nb, T, KV_HEADS, ATT_GROUP, Dh).transpose(1, 0, 2, 3, 4, 5)

    def band(t):
        tp = jnp.pad(t, ((0, 0), (T, T), (0, 0), (0, 0))).reshape(B, nb + 2, T, KV_HEADS, Dh)
        return jnp.concatenate([tp[:, :-2], tp[:, 1:-1], tp[:, 2:]], axis=2).transpose(1, 0, 2, 3, 4)

    kl, vl = band(k), band(v)
    sink_l = sink.astype(jnp.float32).reshape(1, KV_HEADS, ATT_GROUP, 1, 1)
    offs_q = jnp.arange(T)
    offs_k = jnp.arange(3 * T) - T

    def one(args):
        i, qi, ki, vi = args
        qpos = i * T + offs_q
        kpos = i * T + offs_k
        valid = (jnp.abs(qpos[:, None] - kpos[None, :]) <= ATT_WINDOW) & (kpos >= 0)[None, :] & (kpos < N)[None, :]
        s_loc = jnp.einsum('bqkgd,bskd->bkgqs', qi, ki).astype(jnp.float32) * ATT_SCALE
        s_loc = jnp.where(valid, s_loc, NEG_INF)
        s_ctx = jnp.einsum('bqkgd,bckd->bkgqc', qi, kc).astype(jnp.float32) * ATT_SCALE
        p = _sink_softmax(jnp.concatenate([s_loc, s_ctx], axis=-1), sink_l)
        o = (jnp.einsum('bkgqs,bskd->bqkgd', p[..., :3 * T].astype(vi.dtype), vi)
             + jnp.einsum('bkgqc,bckd->bqkgd', p[..., 3 * T:].astype(vc.dtype), vc))
        return o.reshape(B, T, H, Dh)

    o = lax.map(one, (jnp.arange(nb), qb, kl, vl))
    return o.transpose(1, 0, 2, 3, 4).reshape(B, N, H, Dh)


def _hier_moe(h, rgw, rgb, rew, reb, wg, wu, wd):
    B, N, D = h.shape
    T = B * N
    x = h.reshape(T, D)
    g_prob = jax.nn.softmax((x @ rgw).astype(jnp.float32) + rgb.astype(jnp.float32), axis=-1)
    g_p, g_idx = lax.top_k(g_prob, 1)
    e_logits = ((x @ rew).astype(jnp.float32) + reb.astype(jnp.float32)).reshape(T, N_EXPERT_GROUPS, EXPERTS_PER_GROUP)
    e_in = e_logits[jnp.arange(T), g_idx[:, 0]]
    e_val, e_loc = lax.top_k(e_in, EXPERT_TOPK)
    wts = g_p * jax.nn.softmax(e_val, axis=-1)
    eid = g_idx * EXPERTS_PER_GROUP + e_loc

    A = T * EXPERT_TOPK
    e_flat = eid.reshape(A)
    tok_flat = jnp.repeat(jnp.arange(T), EXPERT_TOPK)
    w_flat = wts.reshape(A)
    order = jnp.argsort(e_flat)
    e_s, tok_s, w_s = e_flat[order], tok_flat[order], w_flat[order]
    counts = jnp.bincount(e_flat, length=N_EXPERTS)
    starts = jnp.cumsum(counts) - counts
    padded = ((counts + MOE_BLOCK - 1) // MOE_BLOCK) * MOE_BLOCK
    pends = jnp.cumsum(padded)
    pstarts = pends - padded
    dest = pstarts[e_s] + (jnp.arange(A) - starts[e_s])
    P = A + N_EXPERTS * MOE_BLOCK
    nblk = P // MOE_BLOCK
    buf_tok = jnp.full((P,), T, dtype=jnp.int32).at[dest].set(tok_s.astype(jnp.int32))
    buf_w = jnp.zeros((P,), jnp.float32).at[dest].set(w_s)
    blk_e = jnp.minimum(jnp.searchsorted(pends, jnp.arange(nblk) * MOE_BLOCK, side='right'), N_EXPERTS - 1)
    x_pad = jnp.concatenate([x, jnp.zeros((1, D), x.dtype)], axis=0)
    xb = x_pad[buf_tok].reshape(nblk, MOE_BLOCK, D)

    def expert_block(args):
        xblk, e = args
        hid = jax.nn.silu(xblk @ wg[e]) * (xblk @ wu[e])
        return hid @ wd[e]

    yb = lax.map(expert_block, (xb, blk_e)).reshape(P, D)
    out = jnp.zeros((T + 1, D), x.dtype).at[buf_tok].add(yb * buf_w[:, None].astype(x.dtype))[:T]
    return out.reshape(B, N, D)


def _trunk_layer(x, mods, p, ctx_cache=None):
    sh1, sc1, g1, sh2, sc2, g2 = mods
    B, N, _ = x.shape
    h = _rms_norm(x, p['norm1_g']) * (1 + sc1) + sh1
    ca, cg, rq, rk, rv, rgate, aq, ak, av = jnp.split(h @ p['w_in'], SPLIT_POINTS, axis=-1)
    conv_out = _conv_module(ca, cg, p['conv_dw_w'], p['conv_dw_b'], p['conv_ln_g'], p['conv_ln_b'])
    q = _rms_norm(aq.reshape(B, N, ATT_HEADS, ATT_HD), p['q_norm_g'])
    k = _rms_norm(ak.reshape(B, N, KV_HEADS, ATT_HD), p['k_norm_g'])
    v = av.reshape(B, N, KV_HEADS, ATT_HD)
    if ctx_cache is None:
        s0 = jnp.zeros((B, RET_HEADS, RET_HD, RET_HD), jnp.float32)
        ret_out, s_f, s_b = _retention(rq, rk, rv, rgate, p['ret_decay'], s0, s0)
        att = _context_attention(q, k, v, p['attn_sink'])
        new = (k, v, jnp.stack([s_f, s_b], axis=1))
    else:
        kc, vc, st = ctx_cache
        ret_out, _, _ = _retention(rq, rk, rv, rgate, p['ret_decay'], st[:, 0], st[:, 1])
        att = _latent_attention(_axial_rope(q), _axial_rope(k), v, kc, vc, p['attn_sink'])
        new = None
    mix = jnp.concatenate([conv_out, ret_out, att.reshape(B, N, ATT_HEADS * ATT_HD)], axis=-1) @ p['w_out']
    x = x + g1 * mix
    h2 = _rms_norm(x, p['norm2_g']) * (1 + sc2) + sh2
    x = x + g2 * _hier_moe(h2, p['router_group_w'], p['router_group_b'], p['router_expert_w'],
                           p['router_expert_b'], p['expert_w_gate'], p['expert_w_up'], p['expert_w_down'])
    return x, new


def setup_inputs(seed: int = 0) -> dict:
    key = jax.random.key(seed)
    ks = jax.random.split(key, 32)
    f32 = jnp.float32
    D = D_MODEL

    def nrm(k, shape, s):
        return jax.random.normal(k, shape, f32) * s

    heads = jnp.arange(RET_HEADS, dtype=f32)
    decay_base = jnp.stack([jnp.log(2.0 ** (5.0 + heads) - 1.0), jnp.log(2.0 ** (5.5 + heads) - 1.0)])
    return {
        'x_prompt': nrm(ks[0], (BATCH, SEQ, D), 1.0),
        'x_sample': nrm(ks[1], (DEC_BATCH, DEC_SEQ, D), 1.0),
        'cache_attn_k': nrm(ks[2], (DEC_BATCH, DEPTH, PAST_LEN, KV_HEADS, ATT_HD), 1.0),
        'cache_attn_v': nrm(ks[3], (DEC_BATCH, DEPTH, PAST_LEN, KV_HEADS, ATT_HD), 1.0),
        'state_retention': nrm(ks[4], (DEC_BATCH, DEPTH, 2, RET_HEADS, RET_HD, RET_HD), 0.3),
        'c': nrm(ks[5], (DEC_BATCH, D), 1.0),
        'c_ctx': nrm(ks[6], (D,), 1.0),
        'w_mod': nrm(ks[7], (DEPTH, D, 6 * D), 0.5 * D ** -0.5),
        'b_mod': nrm(ks[8], (DEPTH, 6 * D), 0.02),
        'norm1_g': 1.0 + nrm(ks[9], (DEPTH, D), 0.02),
        'norm2_g': 1.0 + nrm(ks[10], (DEPTH, D), 0.02),
        'w_in': nrm(ks[11], (DEPTH, D, IN_WIDTH), D ** -0.5),
        'conv_dw_w': nrm(ks[12], (DEPTH, CONV_WIDTH, CONV_CH), CONV_WIDTH ** -0.5),
        'conv_dw_b': nrm(ks[13], (DEPTH, CONV_CH), 0.02),
        'conv_ln_g': 1.0 + nrm(ks[14], (DEPTH, CONV_CH), 0.02),
        'conv_ln_b': nrm(ks[15], (DEPTH, CONV_CH), 0.02),
        'ret_decay': decay_base[None] + nrm(ks[16], (DEPTH, 2, RET_HEADS), 0.05),
        'q_norm_g': 1.0 + nrm(ks[17], (DEPTH, ATT_HD), 0.02),
        'k_norm_g': 1.0 + nrm(ks[18], (DEPTH, ATT_HD), 0.02),
        'attn_sink': nrm(ks[19], (DEPTH, ATT_HEADS), 0.5),
        'w_out': nrm(ks[20], (DEPTH, D, D), D ** -0.5),
        'router_group_w': nrm(ks[21], (DEPTH, D, N_EXPERT_GROUPS), D ** -0.5),
        'router_group_b': nrm(ks[22], (DEPTH, N_EXPERT_GROUPS), 0.01),
        'router_expert_w': nrm(ks[23], (DEPTH, D, N_EXPERTS), D ** -0.5),
        'router_expert_b': nrm(ks[24], (DEPTH, N_EXPERTS), 0.01),
        'expert_w_gate': nrm(ks[25], (DEPTH, N_EXPERTS, D, D_EXPERT), D ** -0.5),
        'expert_w_up': nrm(ks[26], (DEPTH, N_EXPERTS, D, D_EXPERT), D ** -0.5),
        'expert_w_down': nrm(ks[27], (DEPTH, N_EXPERTS, D_EXPERT, D), D_EXPERT ** -0.5),
    }


def reference(x_prompt, x_sample, cache_attn_k, cache_attn_v, state_retention, c, c_ctx,
              w_mod, b_mod, norm1_g, norm2_g, w_in, conv_dw_w, conv_dw_b, conv_ln_g, conv_ln_b,
              ret_decay, q_norm_g, k_norm_g, attn_sink, w_out, router_group_w, router_group_b,
              router_expert_w, router_expert_b, expert_w_gate, expert_w_up, expert_w_down):
    y_prompt = x_prompt
    y_sample = x_sample
    k_list, v_list, s_list = [], [], []
    for l in range(DEPTH):
        p = {
            'norm1_g': norm1_g[l], 'norm2_g': norm2_g[l], 'w_in': w_in[l],
            'conv_dw_w': conv_dw_w[l], 'conv_dw_b': conv_dw_b[l], 'conv_ln_g': conv_ln_g[l], 'conv_ln_b': conv_ln_b[l],
            'ret_decay': ret_decay[l], 'q_norm_g': q_norm_g[l], 'k_norm_g': k_norm_g[l], 'attn_sink': attn_sink[l],
            'w_out': w_out[l], 'router_group_w': router_group_w[l], 'router_group_b': router_group_b[l],
            'router_expert_w': router_expert_w[l], 'router_expert_b': router_expert_b[l],
            'expert_w_gate': expert_w_gate[l], 'expert_w_up': expert_w_up[l], 'expert_w_down': expert_w_down[l],
        }
        mod_ctx = _modulation(c_ctx[None, :], w_mod[l], b_mod[l])
        mod_lat = _modulation(c, w_mod[l], b_mod[l])
        y_prompt, (k_l, v_l, s_l) = _trunk_layer(y_prompt, mod_ctx, p)
        k_list.append(k_l)
        v_list.append(v_l)
        s_list.append(s_l)
        y_sample, _ = _trunk_layer(y_sample, mod_lat, p,
                                   (cache_attn_k[:, l], cache_attn_v[:, l], state_retention[:, l]))
    new_cache_attn_k = jnp.stack(k_list, axis=1)
    new_cache_attn_v = jnp.stack(v_list, axis=1)
    new_state_retention = jnp.stack(s_list, axis=1)
    return (y_prompt, y_sample, new_cache_attn_k, new_cache_attn_v, new_state_retention)
```

```python
import functools

import jax
import jax.numpy as jnp
from jax import lax
from jax.experimental import pallas as pl
from jax.experimental.pallas import tpu as pltpu

F32, BF16, I32 = jnp.float32, jnp.bfloat16, jnp.int32

GRID_W = 64
ATT_WINDOW = 128
ROPE_BASE = 10000.0
EXPERT_TOPK = 2
EPS = 1e-6
NEG_INF = -1e30

LANES = 128
SUBLANES = 8
VMEM_LIMIT_BYTES = 56 * 1024 * 1024

MOD_TN = 1024
INPROJ_TM = 1024
INPROJ_TN = 512
CONV_TT = 256
CONV_HALO = 16
CONV_RC = 32
RET_TQ = 256
ATT_TQ = 256
OUT_TM = 256
MOE_TM = 128
CMB_TT = 256


def _cparams(sem):
    return pltpu.CompilerParams(dimension_semantics=sem, vmem_limit_bytes=VMEM_LIMIT_BYTES)


def _rms(x, g):
    return x * lax.rsqrt(jnp.mean(x * x, axis=-1, keepdims=True) + EPS) * g


def _silu(x):
    return x * jax.nn.sigmoid(x)


def _dot(a, b):
    return jnp.dot(a, b, preferred_element_type=F32)


def _dot_nt(a, b):
    return lax.dot_general(a, b, (((1,), (1,)), ((), ())), preferred_element_type=F32)


def _dot_tn(a, b):
    return lax.dot_general(a, b, (((0,), (0,)), ((), ())), preferred_element_type=F32)


def _mod_kernel(cv_ref, w_ref, b_ref, o_ref):
    s = _silu(cv_ref[...]).astype(BF16)
    o_ref[...] = _dot(s, w_ref[...].astype(BF16)) + b_ref[...]


def _modulation(cv, w_mod, b_mod):
    depth, d, n6 = w_mod.shape
    rows = cv.shape[0]
    return pl.pallas_call(
        _mod_kernel,
        out_shape=jax.ShapeDtypeStruct((depth, rows, n6), F32),
        grid=(depth, n6 // MOD_TN),
        in_specs=[
            pl.BlockSpec((rows, d), lambda l, j: (0, 0)),
            pl.BlockSpec((None, d, MOD_TN), lambda l, j: (l, 0, j)),
            pl.BlockSpec((None, 1, MOD_TN), lambda l, j: (l, 0, j)),
        ],
        out_specs=pl.BlockSpec((None, rows, MOD_TN), lambda l, j: (l, 0, j)),
        compiler_params=_cparams(("arbitrary", "arbitrary")),
        name="modulation",
    )(cv, w_mod, b_mod.reshape(depth, 1, n6))


def _inproj_kernel(x_ref, sc_ref, sh_ref, g_ref, w_ref, o_ref, h_ref):
    @pl.when(pl.program_id(1) == 0)
    def _():
        h = _rms(x_ref[...], g_ref[...]) * (1.0 + sc_ref[...]) + sh_ref[...]
        h_ref[...] = h.astype(BF16)

    o_ref[...] = _dot(h_ref[...], w_ref[...].astype(BF16))


def _conv_kernel(a_ref, g_ref, ap_ref, gp_ref, an_ref, gn_ref, w_ref, b_ref, lng_ref, lnb_ref, o_ref,
                 hs_ref, *, nctx, ctx_seq, lat_seq, width):
    t0 = pl.program_id(0) * CONV_TT
    is_ctx = t0 < nctx
    seq_len = jnp.where(is_ctx, ctx_seq, lat_seq)
    pos = lax.rem(jnp.where(is_ctx, t0, t0 - nctx), seq_len)
    first = pos == 0
    last = pos + CONV_TT == seq_len
    rows = CONV_TT + 2 * CONV_HALO
    span = rows - SUBLANES
    hs_ref[0, 0:CONV_HALO, :] = jnp.where(first, 0.0, ap_ref[...] * jax.nn.sigmoid(gp_ref[...]))
    hs_ref[0, CONV_HALO:CONV_HALO + CONV_TT, :] = a_ref[...] * jax.nn.sigmoid(g_ref[...])
    hs_ref[0, CONV_HALO + CONV_TT:rows, :] = jnp.where(last, 0.0, an_ref[...] * jax.nn.sigmoid(gn_ref[...]))
    for s in range(1, SUBLANES):
        hs_ref[s, 0:span, :] = hs_ref[0, s:s + span, :]
    base = CONV_HALO - width // 2
    for r0 in range(0, CONV_TT, CONV_RC):
        acc = jnp.zeros((CONV_RC, o_ref.shape[1]), F32)
        for k in range(width):
            idx = k + base
            s, q = idx % SUBLANES, idx // SUBLANES
            acc = acc + w_ref[k:k + 1, :] * hs_ref[s, r0 + SUBLANES * q:r0 + SUBLANES * q + CONV_RC, :]
        hf = acc + b_ref[...]
        mu = jnp.mean(hf, axis=-1, keepdims=True)
        var = jnp.mean(jnp.square(hf - mu), axis=-1, keepdims=True)
        y = (hf - mu) * lax.rsqrt(var + EPS) * lng_ref[...] + lnb_ref[...]
        o_ref[r0:r0 + CONV_RC, :] = _silu(y).astype(BF16)


def _ret_kernel(lg_ref, q_ref, k_ref, v_ref, gate_ref, *rest, layer, heads, seq, has_s0, emit_state):
    rest = list(rest)
    s0_ref = rest.pop(0) if has_s0 else None
    o_ref = rest.pop(0)
    st_ref = rest.pop(0) if emit_state else None
    h = pl.program_id(1)
    qi = pl.program_id(2)
    lgf = lg_ref[layer * 2 * heads + h]
    lgb = lg_ref[layer * 2 * heads + heads + h]
    dh = q_ref.shape[1]
    q = q_ref[...]
    k = k_ref[...] * (dh ** -0.5)
    vb = v_ref[...].astype(BF16)
    a = _dot_nt(q.astype(BF16), k.astype(BF16))
    n_idx = qi * RET_TQ + lax.broadcasted_iota(I32, a.shape, 0)
    m_idx = lax.broadcasted_iota(I32, a.shape, 1)
    rel = (n_idx - m_idx).astype(F32)
    decay = jnp.exp(jnp.where(rel >= 0.0, lgf, -lgb) * rel)
    o = _dot((a * decay).astype(BF16), vb)
    if has_s0:
        npos = (qi * RET_TQ + lax.broadcasted_iota(I32, (RET_TQ, 1), 0)).astype(F32)
        cf = jnp.exp(lgf * (npos + 1.0))
        cb = jnp.exp(lgb * (seq - npos))
        o = o + _dot((q * cf).astype(BF16), s0_ref[0].astype(BF16)) + _dot((q * cb).astype(BF16), s0_ref[1].astype(BF16))
    y = o * lax.rsqrt(jnp.mean(o * o, axis=-1, keepdims=True) + EPS)
    o_ref[...] = (_silu(gate_ref[...]) * y).astype(BF16)
    if emit_state:
        m = lax.broadcasted_iota(I32, (seq, 1), 0).astype(F32)
        st_ref[0] = _dot_tn((k * jnp.exp(lgf * (seq - 1.0 - m))).astype(BF16), vb)
        st_ref[1] = _dot_tn((k * jnp.exp(lgb * m)).astype(BF16), vb)


def _attn_ctx_kernel(sink_ref, q_ref, k_ref, v_ref, gq_ref, gk_ref, o_ref, ko_ref, vo_ref, *, layer, group, scale):
    kv = pl.program_id(1)
    dh = k_ref.shape[1]
    kn = _rms(k_ref[...], gk_ref[...])
    v = v_ref[...]
    ko_ref[...] = kn
    vo_ref[...] = v
    knb = kn.astype(BF16)
    vb = v.astype(BF16)
    for g in range(group):
        sink = sink_ref[layer * pl.num_programs(1) * group + kv * group + g]
        qn = _rms(q_ref[:, g * dh:(g + 1) * dh], gq_ref[...])
        s = _dot_nt(qn.astype(BF16), knb) * scale
        m = jnp.maximum(jnp.max(s, axis=-1, keepdims=True), sink)
        p = jnp.exp(s - m)
        den = jnp.sum(p, axis=-1, keepdims=True) + jnp.exp(sink - m)
        o_ref[:, g * dh:(g + 1) * dh] = (_dot(p.astype(BF16), vb) / den).astype(BF16)


def _rope(x, cos, sin_signed):
    lane = lax.broadcasted_iota(I32, x.shape, 1)
    half = x.shape[1] // 4
    swapped = jnp.where(lax.rem(lane, 2 * half) < half, pltpu.roll(x, x.shape[1] - half, 1), pltpu.roll(x, half, 1))
    return x * cos + swapped * sin_signed


def _attn_lat_kernel(sink_ref, q_ref, k_ref, v_ref, kc_ref, vc_ref, gq_ref, gk_ref, cos_ref, sin_ref,
                     o_ref, kr_ref, *, layer, group, scale, seq, nkv):
    kv = pl.program_id(1)
    qi = pl.program_id(2)
    dh = k_ref.shape[1]
    win = ATT_TQ + 2 * ATT_WINDOW

    @pl.when(qi == 0)
    def _():
        kr_ref[...] = _rope(_rms(k_ref[...], gk_ref[...]), cos_ref[...], sin_ref[...]).astype(BF16)

    q0 = pl.multiple_of(qi * ATT_TQ, ATT_TQ)
    start = pl.multiple_of(jnp.clip(qi * ATT_TQ - ATT_WINDOW, 0, seq - win), ATT_WINDOW)
    kw = kr_ref[pl.ds(start, win), :]
    vw = v_ref[pl.ds(start, win), :].astype(BF16)
    kcb = kc_ref[...].astype(BF16)
    vcb = vc_ref[...].astype(BF16)
    qpos = q0 + lax.broadcasted_iota(I32, (ATT_TQ, win), 0)
    kpos = start + lax.broadcasted_iota(I32, (ATT_TQ, win), 1)
    valid = jnp.abs(qpos - kpos) <= ATT_WINDOW
    cs = cos_ref[pl.ds(q0, ATT_TQ), :]
    sn = sin_ref[pl.ds(q0, ATT_TQ), :]
    for g in range(group):
        sink = sink_ref[layer * nkv * group + kv * group + g]
        qr = _rope(_rms(q_ref[:, g * dh:(g + 1) * dh], gq_ref[...]), cs, sn).astype(BF16)
        s_loc = jnp.where(valid, _dot_nt(qr, kw) * scale, NEG_INF)
        s_ctx = _dot_nt(qr, kcb) * scale
        m = jnp.maximum(jnp.maximum(jnp.max(s_loc, axis=-1, keepdims=True), jnp.max(s_ctx, axis=-1, keepdims=True)), sink)
        p_loc = jnp.exp(s_loc - m)
        p_ctx = jnp.exp(s_ctx - m)
        den = jnp.sum(p_loc, axis=-1, keepdims=True) + jnp.sum(p_ctx, axis=-1, keepdims=True) + jnp.exp(sink - m)
        o = _dot(p_loc.astype(BF16), vw) + _dot(p_ctx.astype(BF16), vcb)
        o_ref[:, g * dh:(g + 1) * dh] = (o / den).astype(BF16)


def _split_bf16(x):
    hi = x.astype(BF16)
    lo = (x - hi.astype(F32)).astype(BF16)
    return hi, lo


def _outproj_kernel(conv_ref, ret_ref, att_ref, x_ref, g1_ref, sc2_ref, sh2_ref, n2_ref, wout_ref, rw_ref, rb_ref,
                    x1_ref, h2_ref, route_ref, cnt_ref, wb_ref, rwh_ref, rwl_ref, run_ref, *, n_groups, n_experts):
    i = pl.program_id(0)
    c_conv = conv_ref.shape[1]
    c_ret = ret_ref.shape[1]

    @pl.when(i == 0)
    def _():
        wb_ref[...] = wout_ref[...].astype(BF16)
        hi, lo = _split_bf16(rw_ref[...])
        rwh_ref[...] = hi
        rwl_ref[...] = lo
        run_ref[...] = jnp.zeros_like(run_ref)

    mix = (_dot(conv_ref[...], wb_ref[0:c_conv, :]) + _dot(ret_ref[...], wb_ref[c_conv:c_conv + c_ret, :])
           + _dot(att_ref[...], wb_ref[c_conv + c_ret:, :]))
    x1 = x_ref[...] + g1_ref[...] * mix
    x1_ref[...] = x1
    h2 = _rms(x1, n2_ref[...]) * (1.0 + sc2_ref[...]) + sh2_ref[...]
    h2_ref[...] = h2
    hh, hl = _split_bf16(h2)
    lg = _dot(hh, rwh_ref[...]) + _dot(hl, rwh_ref[...]) + _dot(hh, rwl_ref[...]) + rb_ref[...]
    tm = lg.shape[0]
    lane = lax.broadcasted_iota(I32, lg.shape, 1)
    lanef = lane.astype(F32)
    big = float(LANES)
    gmask = (lane >= n_experts) & (lane < n_experts + n_groups)
    gl = jnp.where(gmask, lg, NEG_INF)
    gmax = jnp.max(gl, axis=-1, keepdims=True)
    gidx = jnp.min(jnp.where(gl == gmax, lanef, big), axis=-1, keepdims=True) - float(n_experts)
    gp = 1.0 / jnp.sum(jnp.where(gmask, jnp.exp(gl - gmax), 0.0), axis=-1, keepdims=True)
    per_group = n_experts // n_groups
    emask = (lanef >= gidx * per_group) & (lanef < (gidx + 1.0) * per_group)
    el = jnp.where(emask, lg, NEG_INF)
    v1 = jnp.max(el, axis=-1, keepdims=True)
    i1 = jnp.min(jnp.where(el == v1, lanef, big), axis=-1, keepdims=True)
    el2 = jnp.where(lanef == i1, NEG_INF, el)
    v2 = jnp.max(el2, axis=-1, keepdims=True)
    i2 = jnp.min(jnp.where(el2 == v2, lanef, big), axis=-1, keepdims=True)
    t = jnp.exp(v2 - v1)
    w1 = gp / (1.0 + t)
    w2 = gp * t / (1.0 + t)
    sel1 = lanef == i1
    sel2 = lanef == i2
    oh = jnp.where(sel1 | sel2, 1.0, 0.0)
    r_i = lax.broadcasted_iota(I32, (tm, tm), 0)
    c_i = lax.broadcasted_iota(I32, (tm, tm), 1)
    lower = jnp.where(c_i < r_i, 1.0, 0.0).astype(BF16)
    before = _dot(lower, oh.astype(BF16)) + run_ref[...]
    rank1 = jnp.sum(jnp.where(sel1, before, 0.0), axis=-1, keepdims=True)
    rank2 = jnp.sum(jnp.where(sel2, before, 0.0), axis=-1, keepdims=True)
    run_ref[...] = run_ref[...] + jnp.sum(oh, axis=0, keepdims=True)
    cnt_ref[...] = run_ref[...]
    route = jnp.where(lane == 0, i1, jnp.where(lane == 1, i2, jnp.where(lane == 2, w1, jnp.where(
        lane == 3, w2, jnp.where(lane == 4, rank1, jnp.where(lane == 5, rank2, 0.0))))))
    route_ref[...] = route


def _expert_kernel(blk_e_ref, row_tok_ref, nused_ref, h2_hbm, wg_ref, wu_ref, wd_ref, ys_ref,
                   xbuf, wgb, wub, wdb, sem):
    b = pl.program_id(0)
    nused = nused_ref[0]
    slot = lax.rem(b, 2)

    def gather_start(blk, sl):
        def body(r, carry):
            tok = row_tok_ref[blk * MOE_TM + r]
            pltpu.make_async_copy(h2_hbm.at[pl.ds(tok, 1), :], xbuf.at[sl, pl.ds(r, 1), :], sem.at[sl]).start()
            return carry

        lax.fori_loop(0, MOE_TM, body, 0)

    def gather_wait(sl):
        pltpu.make_async_copy(h2_hbm.at[pl.ds(0, MOE_TM), :], xbuf.at[sl], sem.at[sl]).wait()

    @pl.when((b == 0) & (nused > 0))
    def _():
        gather_start(0, 0)

    @pl.when(b + 1 < nused)
    def _():
        gather_start(b + 1, 1 - slot)

    @pl.when(b < nused)
    def _():
        e = blk_e_ref[b]
        prev = blk_e_ref[jnp.maximum(b - 1, 0)]

        @pl.when((b == 0) | (e != prev))
        def _():
            wgb[...] = wg_ref[...].astype(BF16)
            wub[...] = wu_ref[...].astype(BF16)
            wdb[...] = wd_ref[...].astype(BF16)

        gather_wait(slot)
        x = xbuf[slot].astype(BF16)
        hid = _silu(_dot(x, wgb[...])) * _dot(x, wub[...])
        ys_ref[...] = _dot(hid.astype(BF16), wdb[...])

    @pl.when(b >= nused)
    def _():
        ys_ref[...] = jnp.zeros_like(ys_ref)


def _combine_kernel(d0_ref, d1_ref, ys_hbm, x1_ref, route_ref, g2_ref, o_ref, ybuf, sem):
    i = pl.program_id(0)
    n = pl.num_programs(0)
    slot = lax.rem(i, 2)

    def gather_start(tile, sl):
        def body(r, carry):
            t = tile * CMB_TT + r
            pltpu.make_async_copy(ys_hbm.at[pl.ds(d0_ref[t], 1), :], ybuf.at[sl, 0, pl.ds(r, 1), :], sem.at[sl]).start()
            pltpu.make_async_copy(ys_hbm.at[pl.ds(d1_ref[t], 1), :], ybuf.at[sl, 1, pl.ds(r, 1), :], sem.at[sl]).start()
            return carry

        lax.fori_loop(0, CMB_TT, body, 0)

    def gather_wait(sl):
        for j in range(EXPERT_TOPK):
            pltpu.make_async_copy(ys_hbm.at[pl.ds(0, CMB_TT), :], ybuf.at[sl, j], sem.at[sl]).wait()

    @pl.when(i == 0)
    def _():
        gather_start(0, 0)

    @pl.when(i + 1 < n)
    def _():
        gather_start(i + 1, 1 - slot)

    gather_wait(slot)
    w0 = route_ref[:, 2:3]
    w1 = route_ref[:, 3:4]
    moe = ybuf[slot, 0] * w0 + ybuf[slot, 1] * w1
    o_ref[...] = x1_ref[...] + g2_ref[...] * moe


def _rope_tables(seq, dh):
    nf = dh // 4
    t = jnp.arange(seq)
    inv = ROPE_BASE ** (-jnp.arange(nf, dtype=F32) / nf)
    ang_r = (t // GRID_W).astype(F32)[:, None] * inv[None, :]
    ang_c = (t % GRID_W).astype(F32)[:, None] * inv[None, :]
    cos = jnp.concatenate([jnp.cos(ang_r)] * 2 + [jnp.cos(ang_c)] * 2, axis=-1)
    sin = jnp.concatenate([-jnp.sin(ang_r), jnp.sin(ang_r), -jnp.sin(ang_c), jnp.sin(ang_c)], axis=-1)
    return cos, sin


def kernel(x_prompt, x_sample, cache_attn_k, cache_attn_v, state_retention, c, c_ctx, w_mod, b_mod, norm1_g, norm2_g, w_in, conv_dw_w, conv_dw_b, conv_ln_g, conv_ln_b, ret_decay, q_norm_g, k_norm_g, attn_sink, w_out, router_group_w, router_group_b, router_expert_w, router_expert_b, expert_w_gate, expert_w_up, expert_w_down):
    batch, ctx_seq, d = x_prompt.shape
    dec_batch, lat_seq, _ = x_sample.shape
    depth = w_mod.shape[0]
    in_width = w_in.shape[2]
    conv_width, conv_ch = conv_dw_w.shape[1:]
    ret_heads = ret_decay.shape[2]
    ret_hd = state_retention.shape[-1]
    att_heads = attn_sink.shape[1]
    kv_heads, att_hd = cache_attn_k.shape[3:]
    past_len = cache_attn_k.shape[2]
    group = att_heads // kv_heads
    n_groups = router_group_w.shape[2]
    n_experts = router_expert_w.shape[2]
    d_expert = expert_w_gate.shape[3]
    nctx, nlat = batch * ctx_seq, dec_batch * lat_seq
    nt = nctx + nlat
    ret_w = ret_heads * ret_hd
    att_w = att_heads * att_hd
    kv_w = kv_heads * att_hd
    c_ca, c_cg = 0, conv_ch
    c_rq = 2 * conv_ch
    c_rk, c_rv, c_rg = c_rq + ret_w, c_rq + 2 * ret_w, c_rq + 3 * ret_w
    c_aq = c_rq + 4 * ret_w
    c_ak = c_aq + att_w
    c_av = c_ak + kv_w
    assert c_av + kv_w == in_width
    assert ctx_seq == CONV_TT == RET_TQ and lat_seq % ATT_TQ == 0 and INPROJ_TM == lat_seq
    assert conv_width // 2 <= CONV_HALO and n_experts + n_groups <= LANES

    mod_rows = SUBLANES
    assert 1 + dec_batch <= mod_rows
    cv = jnp.concatenate([c_ctx[None, :], c, jnp.zeros((mod_rows - 1 - dec_batch, d), F32)], axis=0)
    mods = _modulation(cv, w_mod, b_mod).reshape(depth, mod_rows, 1, 6 * d)

    def mod_row(t0):
        return jnp.where(t0 < nctx, 0, (t0 - nctx) // lat_seq + 1)

    def mod_spec(layer, chunk, tile):
        return pl.BlockSpec((None, None, 1, d), lambda i, *_: (layer, mod_row(i * tile), 0, chunk))

    lg = jax.nn.log_sigmoid(ret_decay.astype(F32)).reshape(-1)
    sinks = attn_sink.astype(F32).reshape(-1)
    cos_t, sin_t = _rope_tables(lat_seq, att_hd)
    rw = jnp.concatenate([router_expert_w, router_group_w, jnp.zeros((depth, d, LANES - n_experts - n_groups), F32)], axis=2)
    rb = jnp.concatenate([router_expert_b, router_group_b, jnp.zeros((depth, LANES - n_experts - n_groups), F32)], axis=1)
    rb = rb.reshape(depth, 1, LANES)
    kc_all = cache_attn_k.reshape(dec_batch, depth, past_len, kv_w)
    vc_all = cache_attn_v.reshape(dec_batch, depth, past_len, kv_w)
    n_assign = nt * EXPERT_TOPK
    p_rows = n_assign + n_experts * MOE_TM
    n_blk = p_rows // MOE_TM
    smem = pl.BlockSpec(memory_space=pltpu.SMEM)
    hbm = pl.BlockSpec(memory_space=pl.ANY)

    x = jnp.concatenate([x_prompt.reshape(nctx, d), x_sample.reshape(nlat, d)], axis=0)
    k_list, v_list, s_list = [], [], []
    for l in range(depth):
        proj = pl.pallas_call(
            _inproj_kernel,
            out_shape=jax.ShapeDtypeStruct((nt, in_width), F32),
            grid=(nt // INPROJ_TM, in_width // INPROJ_TN),
            in_specs=[
                pl.BlockSpec((INPROJ_TM, d), lambda i, j: (i, 0)),
                pl.BlockSpec((None, None, 1, d), lambda i, j, l=l: (l, mod_row(i * INPROJ_TM), 0, 1)),
                pl.BlockSpec((None, None, 1, d), lambda i, j, l=l: (l, mod_row(i * INPROJ_TM), 0, 0)),
                pl.BlockSpec((None, 1, d), lambda i, j, l=l: (l, 0, 0)),
                pl.BlockSpec((None, d, INPROJ_TN), lambda i, j, l=l: (l, 0, j)),
            ],
            out_specs=pl.BlockSpec((INPROJ_TM, INPROJ_TN), lambda i, j: (i, j)),
            scratch_shapes=[pltpu.VMEM((INPROJ_TM, d), BF16)],
            compiler_params=_cparams(("arbitrary", "arbitrary")),
            name="inproj",
        )(x, mods, mods, norm1_g.reshape(depth, 1, d), w_in)

        hb = CONV_TT // CONV_HALO
        n_hb = nt // CONV_HALO
        conv_out = pl.pallas_call(
            functools.partial(_conv_kernel, nctx=nctx, ctx_seq=ctx_seq, lat_seq=lat_seq, width=conv_width),
            out_shape=jax.ShapeDtypeStruct((nt, conv_ch), BF16),
            grid=(nt // CONV_TT,),
            in_specs=[
                pl.BlockSpec((CONV_TT, conv_ch), lambda i: (i, c_ca // conv_ch)),
                pl.BlockSpec((CONV_TT, conv_ch), lambda i: (i, c_cg // conv_ch)),
                pl.BlockSpec((CONV_HALO, conv_ch), lambda i: (jnp.maximum(i * hb - 1, 0), c_ca // conv_ch)),
                pl.BlockSpec((CONV_HALO, conv_ch), lambda i: (jnp.maximum(i * hb - 1, 0), c_cg // conv_ch)),
                pl.BlockSpec((CONV_HALO, conv_ch), lambda i: (jnp.minimum((i + 1) * hb, n_hb - 1), c_ca // conv_ch)),
                pl.BlockSpec((CONV_HALO, conv_ch), lambda i: (jnp.minimum((i + 1) * hb, n_hb - 1), c_cg // conv_ch)),
                pl.BlockSpec((None, conv_width, conv_ch), lambda i, l=l: (l, 0, 0)),
                pl.BlockSpec((None, 1, conv_ch), lambda i, l=l: (l, 0, 0)),
                pl.BlockSpec((None, 1, conv_ch), lambda i, l=l: (l, 0, 0)),
                pl.BlockSpec((None, 1, conv_ch), lambda i, l=l: (l, 0, 0)),
            ],
            out_specs=pl.BlockSpec((CONV_TT, conv_ch), lambda i: (i, 0)),
            scratch_shapes=[pltpu.VMEM((SUBLANES, CONV_TT + 2 * CONV_HALO, conv_ch), F32)],
            compiler_params=_cparams(("arbitrary",)),
            name="conv",
        )(proj, proj, proj, proj, proj, proj, conv_dw_w, conv_dw_b.reshape(depth, 1, conv_ch),
          conv_ln_g.reshape(depth, 1, conv_ch), conv_ln_b.reshape(depth, 1, conv_ch))

        def ret_call(nseq, seq, row0, has_s0, emit_state):
            nq = seq // RET_TQ
            in_specs = [
                smem,
                pl.BlockSpec((RET_TQ, ret_hd), lambda b, h, qi: (row0 // RET_TQ + b * nq + qi, c_rq // ret_hd + h)),
                pl.BlockSpec((seq, ret_hd), lambda b, h, qi: (row0 // seq + b, c_rk // ret_hd + h)),
                pl.BlockSpec((seq, ret_hd), lambda b, h, qi: (row0 // seq + b, c_rv // ret_hd + h)),
                pl.BlockSpec((RET_TQ, ret_hd), lambda b, h, qi: (row0 // RET_TQ + b * nq + qi, c_rg // ret_hd + h)),
            ]
            args = [lg, proj, proj, proj, proj]
            if has_s0:
                in_specs.append(pl.BlockSpec((None, None, 2, None, ret_hd, ret_hd), lambda b, h, qi, l=l: (b, l, 0, h, 0, 0)))
                args.append(state_retention)
            out_shape = [jax.ShapeDtypeStruct((nseq * seq, ret_w), BF16)]
            out_specs = [pl.BlockSpec((RET_TQ, ret_hd), lambda b, h, qi: (b * nq + qi, h))]
            if emit_state:
                out_shape.append(jax.ShapeDtypeStruct((nseq, 2, ret_heads, ret_hd, ret_hd), F32))
                out_specs.append(pl.BlockSpec((None, 2, None, ret_hd, ret_hd), lambda b, h, qi: (b, 0, h, 0, 0)))
            return pl.pallas_call(
                functools.partial(_ret_kernel, layer=l, heads=ret_heads, seq=seq, has_s0=has_s0, emit_state=emit_state),
                out_shape=out_shape,
                grid=(nseq, ret_heads, nq),
                in_specs=in_specs,
                out_specs=out_specs,
                compiler_params=_cparams(("arbitrary", "arbitrary", "arbitrary")),
                name="retention",
            )(*args)

        ret_ctx, s_l = ret_call(batch, ctx_seq, 0, False, True)
        (ret_lat,) = ret_call(dec_batch, lat_seq, nctx, True, False)
        ret_out = jnp.concatenate([ret_ctx, ret_lat], axis=0)
        s_list.append(s_l)

        gq = q_norm_g.reshape(depth, 1, att_hd)
        gk = k_norm_g.reshape(depth, 1, att_hd)
        qw = group * att_hd
        att_ctx, k_l, v_l = pl.pallas_call(
            functools.partial(_attn_ctx_kernel, layer=l, group=group, scale=att_hd ** -0.5),
            out_shape=[jax.ShapeDtypeStruct((nctx, att_w), BF16),
                       jax.ShapeDtypeStruct((batch, ctx_seq, kv_w), F32),
                       jax.ShapeDtypeStruct((batch, ctx_seq, kv_w), F32)],
            grid=(batch, kv_heads),
            in_specs=[
                smem,
                pl.BlockSpec((ctx_seq, qw), lambda b, kv: (b, c_aq // qw + kv)),
                pl.BlockSpec((ctx_seq, att_hd), lambda b, kv: (b, c_ak // att_hd + kv)),
                pl.BlockSpec((ctx_seq, att_hd), lambda b, kv: (b, c_av // att_hd + kv)),
                pl.BlockSpec((None, 1, att_hd), lambda b, kv, l=l: (l, 0, 0)),
                pl.BlockSpec((None, 1, att_hd), lambda b, kv, l=l: (l, 0, 0)),
            ],
            out_specs=[
                pl.BlockSpec((ctx_seq, qw), lambda b, kv: (b, kv)),
                pl.BlockSpec((None, ctx_seq, att_hd), lambda b, kv: (b, 0, kv)),
                pl.BlockSpec((None, ctx_seq, att_hd), lambda b, kv: (b, 0, kv)),
            ],
            compiler_params=_cparams(("arbitrary", "arbitrary")),
            name="attn_ctx",
        )(sinks, proj, proj, proj, gq, gk)
        k_list.append(k_l)
        v_list.append(v_l)

        nq = lat_seq // ATT_TQ
        att_lat = pl.pallas_call(
            functools.partial(_attn_lat_kernel, layer=l, group=group, scale=att_hd ** -0.5, seq=lat_seq, nkv=kv_heads),
            out_shape=jax.ShapeDtypeStruct((nlat, att_w), BF16),
            grid=(dec_batch, kv_heads, nq),
            in_specs=[
                smem,
                pl.BlockSpec((ATT_TQ, qw), lambda b, kv, qi: (nctx // ATT_TQ + b * nq + qi, c_aq // qw + kv)),
                pl.BlockSpec((lat_seq, att_hd), lambda b, kv, qi: (nctx // lat_seq + b, c_ak // att_hd + kv)),
                pl.BlockSpec((lat_seq, att_hd), lambda b, kv, qi: (nctx // lat_seq + b, c_av // att_hd + kv)),
                pl.BlockSpec((None, None, past_len, att_hd), lambda b, kv, qi, l=l: (b, l, 0, kv)),
                pl.BlockSpec((None, None, past_len, att_hd), lambda b, kv, qi, l=l: (b, l, 0, kv)),
                pl.BlockSpec((None, 1, att_hd), lambda b, kv, qi, l=l: (l, 0, 0)),
                pl.BlockSpec((None, 1, att_hd), lambda b, kv, qi, l=l: (l, 0, 0)),
                pl.BlockSpec((lat_seq, att_hd), lambda b, kv, qi: (0, 0)),
                pl.BlockSpec((lat_seq, att_hd), lambda b, kv, qi: (0, 0)),
            ],
            out_specs=pl.BlockSpec((ATT_TQ, qw), lambda b, kv, qi: (b * nq + qi, kv)),
            scratch_shapes=[pltpu.VMEM((lat_seq, att_hd), BF16)],
            compiler_params=_cparams(("arbitrary", "arbitrary", "arbitrary")),
            name="attn_lat",
        )(sinks, proj, proj, proj, kc_all, vc_all, gq, gk, cos_t, sin_t)
        att_out = jnp.concatenate([att_ctx, att_lat], axis=0)

        x1, h2, route, cnt = pl.pallas_call(
            functools.partial(_outproj_kernel, n_groups=n_groups, n_experts=n_experts),
            out_shape=[jax.ShapeDtypeStruct((nt, d), F32), jax.ShapeDtypeStruct((nt, d), F32),
                       jax.ShapeDtypeStruct((nt, LANES), F32), jax.ShapeDtypeStruct((1, LANES), F32)],
            grid=(nt // OUT_TM,),
            in_specs=[
                pl.BlockSpec((OUT_TM, conv_ch), lambda i: (i, 0)),
                pl.BlockSpec((OUT_TM, ret_w), lambda i: (i, 0)),
                pl.BlockSpec((OUT_TM, att_w), lambda i: (i, 0)),
                pl.BlockSpec((OUT_TM, d), lambda i: (i, 0)),
                mod_spec(l, 2, OUT_TM),
                mod_spec(l, 4, OUT_TM),
                mod_spec(l, 3, OUT_TM),
                pl.BlockSpec((None, 1, d), lambda i, l=l: (l, 0, 0)),
                pl.BlockSpec((None, d, d), lambda i, l=l: (l, 0, 0), pipeline_mode=pl.Buffered(1)),
                pl.BlockSpec((None, d, LANES), lambda i, l=l: (l, 0, 0), pipeline_mode=pl.Buffered(1)),
                pl.BlockSpec((None, 1, LANES), lambda i, l=l: (l, 0, 0)),
            ],
            out_specs=[
                pl.BlockSpec((OUT_TM, d), lambda i: (i, 0)),
                pl.BlockSpec((OUT_TM, d), lambda i: (i, 0)),
                pl.BlockSpec((OUT_TM, LANES), lambda i: (i, 0)),
                pl.BlockSpec((1, LANES), lambda i: (0, 0)),
            ],
            scratch_shapes=[pltpu.VMEM((d, d), BF16), pltpu.VMEM((d, LANES), BF16), pltpu.VMEM((d, LANES), BF16),
                            pltpu.VMEM((1, LANES), F32)],
            compiler_params=_cparams(("arbitrary",)),
            name="outproj_router",
        )(conv_out, ret_out, att_out, x, mods, mods, mods, norm2_g.reshape(depth, 1, d), w_out, rw, rb)

        eid = route[:, 0:EXPERT_TOPK].astype(I32)
        rank = route[:, 4:4 + EXPERT_TOPK].astype(I32)
        counts = cnt[0, :n_experts].astype(I32)
        padded = ((counts + MOE_TM - 1) // MOE_TM) * MOE_TM
        pends = jnp.cumsum(padded)
        pstarts = pends - padded
        dest = pstarts[eid] + rank
        tok = jnp.broadcast_to(jnp.arange(nt, dtype=I32)[:, None], (nt, EXPERT_TOPK))
        row_tok = jnp.zeros((p_rows,), I32).at[dest.reshape(-1)].set(tok.reshape(-1))
        blk_e = jnp.minimum(jnp.searchsorted(pends, jnp.arange(n_blk, dtype=I32) * MOE_TM, side='right'), n_experts - 1).astype(I32)
        nused = (pends[-1:] // MOE_TM).astype(I32)

        ys = pl.pallas_call(
            _expert_kernel,
            out_shape=jax.ShapeDtypeStruct((p_rows, d), F32),
            grid_spec=pltpu.PrefetchScalarGridSpec(
                num_scalar_prefetch=3,
                grid=(n_blk,),
                in_specs=[
                    hbm,
                    pl.BlockSpec((None, None, d, d_expert), lambda b, be, rt, nu, l=l: (l, be[b], 0, 0)),
                    pl.BlockSpec((None, None, d, d_expert), lambda b, be, rt, nu, l=l: (l, be[b], 0, 0)),
                    pl.BlockSpec((None, None, d_expert, d), lambda b, be, rt, nu, l=l: (l, be[b], 0, 0)),
                ],
                out_specs=pl.BlockSpec((MOE_TM, d), lambda b, be, rt, nu: (b, 0)),
                scratch_shapes=[pltpu.VMEM((2, MOE_TM, d), F32), pltpu.VMEM((d, d_expert), BF16),
                                pltpu.VMEM((d, d_expert), BF16), pltpu.VMEM((d_expert, d), BF16),
                                pltpu.SemaphoreType.DMA((2,))],
            ),
            compiler_params=_cparams(("arbitrary",)),
            name="experts",
        )(blk_e, row_tok, nused, h2, expert_w_gate, expert_w_up, expert_w_down)

        x = pl.pallas_call(
            _combine_kernel,
            out_shape=jax.ShapeDtypeStruct((nt, d), F32),
            grid_spec=pltpu.PrefetchScalarGridSpec(
                num_scalar_prefetch=2,
                grid=(nt // CMB_TT,),
                in_specs=[
                    hbm,
                    pl.BlockSpec((CMB_TT, d), lambda i, d0, d1: (i, 0)),
                    pl.BlockSpec((CMB_TT, LANES), lambda i, d0, d1: (i, 0)),
                    pl.BlockSpec((None, None, 1, d), lambda i, d0, d1, l=l: (l, mod_row(i * CMB_TT), 0, 5)),
                ],
                out_specs=pl.BlockSpec((CMB_TT, d), lambda i, d0, d1: (i, 0)),
                scratch_shapes=[pltpu.VMEM((2, EXPERT_TOPK, CMB_TT, d), F32), pltpu.SemaphoreType.DMA((2,))],
            ),
            compiler_params=_cparams(("arbitrary",)),
            name="combine",
        )(dest[:, 0], dest[:, 1], ys, x1, route, mods)

    y_prompt = x[:nctx].reshape(batch, ctx_seq, d)
    y_sample = x[nctx:].reshape(dec_batch, lat_seq, d)
    new_k = jnp.stack(k_list, axis=1).reshape(batch, depth, ctx_seq, kv_heads, att_hd)
    new_v = jnp.stack(v_list, axis=1).reshape(batch, depth, ctx_seq, kv_heads, att_hd)
    new_s = jnp.stack(s_list, axis=1)
    return (y_prompt, y_sample, new_k, new_v, new_s)
```

```python
import functools

import jax
import jax.numpy as jnp
from jax import lax
from jax.experimental import pallas as pl
from jax.experimental.pallas import tpu as pltpu

F32, BF16, I32 = jnp.float32, jnp.bfloat16, jnp.int32

GRID_W = 64
ATT_WINDOW = 128
ROPE_BASE = 10000.0
EXPERT_TOPK = 2
EPS = 1e-6
NEG_INF = -1e30

LANES = 128
SUBLANES = 8
VMEM_LIMIT_BYTES = 56 * 1024 * 1024

MOD_TN = 1024
INPROJ_TM = 1024
INPROJ_TN = 512
CONV_TT = 256
CONV_HALO = 16
CONV_RC = 32
RET_TQ = 256
ATT_TQ = 256
OUT_TM = 256
MOE_TM = 128
CMB_TT = 256
GATHER_UNROLL = 8


def _cparams(sem):
    return pltpu.CompilerParams(dimension_semantics=sem, vmem_limit_bytes=VMEM_LIMIT_BYTES)


def _rms(x, g):
    return x * lax.rsqrt(jnp.mean(x * x, axis=-1, keepdims=True) + EPS) * g


def _silu(x):
    return x * jax.nn.sigmoid(x)


def _dot(a, b):
    return jnp.dot(a, b, preferred_element_type=F32)


def _dot_nt(a, b):
    return lax.dot_general(a, b, (((1,), (1,)), ((), ())), preferred_element_type=F32)


def _dot_tn(a, b):
    return lax.dot_general(a, b, (((0,), (0,)), ((), ())), preferred_element_type=F32)


def _mod_kernel(cv_ref, w_ref, b_ref, o_ref):
    s = _silu(cv_ref[...]).astype(BF16)
    o_ref[...] = _dot(s, w_ref[...].astype(BF16)) + b_ref[...]


def _modulation(cv, w_mod, b_mod):
    depth, d, n6 = w_mod.shape
    rows = cv.shape[0]
    return pl.pallas_call(
        _mod_kernel,
        out_shape=jax.ShapeDtypeStruct((depth, rows, n6), F32),
        grid=(depth, n6 // MOD_TN),
        in_specs=[
            pl.BlockSpec((rows, d), lambda l, j: (0, 0)),
            pl.BlockSpec((None, d, MOD_TN), lambda l, j: (l, 0, j)),
            pl.BlockSpec((None, 1, MOD_TN), lambda l, j: (l, 0, j)),
        ],
        out_specs=pl.BlockSpec((None, rows, MOD_TN), lambda l, j: (l, 0, j)),
        compiler_params=_cparams(("arbitrary", "arbitrary")),
        name="modulation",
    )(cv, w_mod, b_mod.reshape(depth, 1, n6))


def _inproj_kernel(x_ref, sc_ref, sh_ref, g_ref, w_ref, o_ref, h_ref, wb_ref):
    i = pl.program_id(0)
    j = pl.program_id(1)

    @pl.when(i == 0)
    def _():
        wb_ref[j] = w_ref[...].astype(BF16)

    @pl.when(j == 0)
    def _():
        h = _rms(x_ref[...], g_ref[...]) * (1.0 + sc_ref[...]) + sh_ref[...]
        h_ref[...] = h.astype(BF16)

    o_ref[...] = _dot(h_ref[...], wb_ref[j]).astype(BF16)


def _conv_kernel(a_ref, g_ref, ap_ref, gp_ref, an_ref, gn_ref, w_ref, b_ref, lng_ref, lnb_ref, o_ref,
                 hs_ref, *, nctx, ctx_seq, lat_seq, width):
    t0 = pl.program_id(0) * CONV_TT
    is_ctx = t0 < nctx
    seq_len = jnp.where(is_ctx, ctx_seq, lat_seq)
    pos = lax.rem(jnp.where(is_ctx, t0, t0 - nctx), seq_len)
    first = pos == 0
    last = pos + CONV_TT == seq_len
    rows = CONV_TT + 2 * CONV_HALO
    span = rows - SUBLANES
    def glu(a, g):
        return a[...].astype(F32) * jax.nn.sigmoid(g[...].astype(F32))

    hs_ref[0, 0:CONV_HALO, :] = jnp.where(first, 0.0, glu(ap_ref, gp_ref))
    hs_ref[0, CONV_HALO:CONV_HALO + CONV_TT, :] = glu(a_ref, g_ref)
    hs_ref[0, CONV_HALO + CONV_TT:rows, :] = jnp.where(last, 0.0, glu(an_ref, gn_ref))
    for s in range(1, SUBLANES):
        hs_ref[s, 0:span, :] = hs_ref[0, s:s + span, :]
    base = CONV_HALO - width // 2
    for r0 in range(0, CONV_TT, CONV_RC):
        acc = jnp.zeros((CONV_RC, o_ref.shape[1]), F32)
        for k in range(width):
            idx = k + base
            s, q = idx % SUBLANES, idx // SUBLANES
            acc = acc + w_ref[k:k + 1, :] * hs_ref[s, r0 + SUBLANES * q:r0 + SUBLANES * q + CONV_RC, :]
        hf = acc + b_ref[...]
        mu = jnp.mean(hf, axis=-1, keepdims=True)
        var = jnp.mean(jnp.square(hf - mu), axis=-1, keepdims=True)
        y = (hf - mu) * lax.rsqrt(var + EPS) * lng_ref[...] + lnb_ref[...]
        o_ref[r0:r0 + CONV_RC, :] = _silu(y).astype(BF16)


def _ret_kernel(lg_ref, q_ref, k_ref, v_ref, gate_ref, *rest, layer, heads, seq, has_s0, emit_state):
    rest = list(rest)
    s0_ref = rest.pop(0) if has_s0 else None
    o_ref = rest.pop(0)
    st_ref = rest.pop(0) if emit_state else None
    h = pl.program_id(1)
    qi = pl.program_id(2)
    lgf = lg_ref[layer * 2 * heads + h]
    lgb = lg_ref[layer * 2 * heads + heads + h]
    dh = q_ref.shape[1]
    q = q_ref[...].astype(F32)
    k = k_ref[...].astype(F32) * (dh ** -0.5)
    vb = v_ref[...]
    a = _dot_nt(q_ref[...], k.astype(BF16))
    n_idx = qi * RET_TQ + lax.broadcasted_iota(I32, a.shape, 0)
    m_idx = lax.broadcasted_iota(I32, a.shape, 1)
    rel = (n_idx - m_idx).astype(F32)
    decay = jnp.exp(jnp.where(rel >= 0.0, lgf, -lgb) * rel)
    o = _dot((a * decay).astype(BF16), vb)
    if has_s0:
        npos = (qi * RET_TQ + lax.broadcasted_iota(I32, (RET_TQ, 1), 0)).astype(F32)
        cf = jnp.exp(lgf * (npos + 1.0))
        cb = jnp.exp(lgb * (seq - npos))
        o = o + _dot((q * cf).astype(BF16), s0_ref[0].astype(BF16)) + _dot((q * cb).astype(BF16), s0_ref[1].astype(BF16))
    y = o * lax.rsqrt(jnp.mean(o * o, axis=-1, keepdims=True) + EPS)
    o_ref[...] = (_silu(gate_ref[...].astype(F32)) * y).astype(BF16)
    if emit_state:
        m = lax.broadcasted_iota(I32, (seq, 1), 0).astype(F32)
        st_ref[0] = _dot_tn((k * jnp.exp(lgf * (seq - 1.0 - m))).astype(BF16), vb)
        st_ref[1] = _dot_tn((k * jnp.exp(lgb * m)).astype(BF16), vb)


def _attn_ctx_kernel(sink_ref, q_ref, k_ref, v_ref, gq_ref, gk_ref, o_ref, ko_ref, vo_ref, *, layer, group, scale):
    kv = pl.program_id(1)
    dh = k_ref.shape[1]
    kn = _rms(k_ref[...].astype(F32), gk_ref[...])
    vb = v_ref[...]
    ko_ref[...] = kn
    vo_ref[...] = vb.astype(F32)
    knb = kn.astype(BF16)
    for g in range(group):
        sink = sink_ref[layer * pl.num_programs(1) * group + kv * group + g]
        qn = _rms(q_ref[:, g * dh:(g + 1) * dh].astype(F32), gq_ref[...])
        s = _dot_nt(qn.astype(BF16), knb) * scale
        m = jnp.maximum(jnp.max(s, axis=-1, keepdims=True), sink)
        p = jnp.exp(s - m)
        den = jnp.sum(p, axis=-1, keepdims=True) + jnp.exp(sink - m)
        o_ref[:, g * dh:(g + 1) * dh] = (_dot(p.astype(BF16), vb) / den).astype(BF16)


def _rope(x, cos, sin_signed):
    lane = lax.broadcasted_iota(I32, x.shape, 1)
    half = x.shape[1] // 4
    swapped = jnp.where(lax.rem(lane, 2 * half) < half, pltpu.roll(x, x.shape[1] - half, 1), pltpu.roll(x, half, 1))
    return x * cos + swapped * sin_signed


def _attn_lat_kernel(sink_ref, q_ref, k_ref, v_ref, kc_ref, vc_ref, gq_ref, gk_ref, cos_ref, sin_ref,
                     o_ref, kr_ref, *, layer, group, scale, seq, nkv):
    kv = pl.program_id(1)
    qi = pl.program_id(2)
    dh = k_ref.shape[1]
    win = ATT_TQ + 2 * ATT_WINDOW

    @pl.when(qi == 0)
    def _():
        kr_ref[...] = _rope(_rms(k_ref[...].astype(F32), gk_ref[...]), cos_ref[...], sin_ref[...]).astype(BF16)

    q0 = pl.multiple_of(qi * ATT_TQ, ATT_TQ)
    start = pl.multiple_of(jnp.clip(qi * ATT_TQ - ATT_WINDOW, 0, seq - win), ATT_WINDOW)
    kw = kr_ref[pl.ds(start, win), :]
    vw = v_ref[pl.ds(start, win), :]
    kcb = kc_ref[...].astype(BF16)
    vcb = vc_ref[...].astype(BF16)
    qpos = q0 + lax.broadcasted_iota(I32, (ATT_TQ, win), 0)
    kpos = start + lax.broadcasted_iota(I32, (ATT_TQ, win), 1)
    valid = jnp.abs(qpos - kpos) <= ATT_WINDOW
    cs = cos_ref[pl.ds(q0, ATT_TQ), :]
    sn = sin_ref[pl.ds(q0, ATT_TQ), :]
    for g in range(group):
        sink = sink_ref[layer * nkv * group + kv * group + g]
        qr = _rope(_rms(q_ref[:, g * dh:(g + 1) * dh].astype(F32), gq_ref[...]), cs, sn).astype(BF16)
        s_loc = jnp.where(valid, _dot_nt(qr, kw) * scale, NEG_INF)
        s_ctx = _dot_nt(qr, kcb) * scale
        m = jnp.maximum(jnp.maximum(jnp.max(s_loc, axis=-1, keepdims=True), jnp.max(s_ctx, axis=-1, keepdims=True)), sink)
        p_loc = jnp.exp(s_loc - m)
        p_ctx = jnp.exp(s_ctx - m)
        den = jnp.sum(p_loc, axis=-1, keepdims=True) + jnp.sum(p_ctx, axis=-1, keepdims=True) + jnp.exp(sink - m)
        o = _dot(p_loc.astype(BF16), vw) + _dot(p_ctx.astype(BF16), vcb)
        o_ref[:, g * dh:(g + 1) * dh] = (o / den).astype(BF16)


def _split_bf16(x):
    hi = x.astype(BF16)
    lo = (x - hi.astype(F32)).astype(BF16)
    return hi, lo


def _outproj_kernel(conv_ref, retc_ref, retl_ref, attc_ref, attl_ref, x_ref, g1_ref, sc2_ref, sh2_ref, n2_ref, wout_ref,
                    rw_ref, rb_ref, x1_ref, h2_ref, route_ref, cnt_ref, wb_ref, rwh_ref, rwl_ref, run_ref,
                    *, n_groups, n_experts, ctx_tiles):
    i = pl.program_id(0)
    c_conv = conv_ref.shape[1]
    c_ret = retc_ref.shape[1]
    is_ctx = i < ctx_tiles
    ret = jnp.where(is_ctx, retc_ref[...], retl_ref[...])
    att = jnp.where(is_ctx, attc_ref[...], attl_ref[...])

    @pl.when(i == 0)
    def _():
        wb_ref[...] = wout_ref[...].astype(BF16)
        hi, lo = _split_bf16(rw_ref[...])
        rwh_ref[...] = hi
        rwl_ref[...] = lo
        run_ref[...] = jnp.zeros_like(run_ref)

    mix = (_dot(conv_ref[...], wb_ref[0:c_conv, :]) + _dot(ret, wb_ref[c_conv:c_conv + c_ret, :])
           + _dot(att, wb_ref[c_conv + c_ret:, :]))
    x1 = x_ref[...] + g1_ref[...] * mix
    x1_ref[...] = x1
    h2 = _rms(x1, n2_ref[...]) * (1.0 + sc2_ref[...]) + sh2_ref[...]
    h2_ref[...] = h2
    hh, hl = _split_bf16(h2)
    lg = _dot(hh, rwh_ref[...]) + _dot(hl, rwh_ref[...]) + _dot(hh, rwl_ref[...]) + rb_ref[...]
    tm = lg.shape[0]
    lane = lax.broadcasted_iota(I32, lg.shape, 1)
    lanef = lane.astype(F32)
    big = float(LANES)
    gmask = (lane >= n_experts) & (lane < n_experts + n_groups)
    gl = jnp.where(gmask, lg, NEG_INF)
    gmax = jnp.max(gl, axis=-1, keepdims=True)
    gidx = jnp.min(jnp.where(gl == gmax, lanef, big), axis=-1, keepdims=True) - float(n_experts)
    gp = 1.0 / jnp.sum(jnp.where(gmask, jnp.exp(gl - gmax), 0.0), axis=-1, keepdims=True)
    per_group = n_experts // n_groups
    emask = (lanef >= gidx * per_group) & (lanef < (gidx + 1.0) * per_group)
    el = jnp.where(emask, lg, NEG_INF)
    v1 = jnp.max(el, axis=-1, keepdims=True)
    i1 = jnp.min(jnp.where(el == v1, lanef, big), axis=-1, keepdims=True)
    el2 = jnp.where(lanef == i1, NEG_INF, el)
    v2 = jnp.max(el2, axis=-1, keepdims=True)
    i2 = jnp.min(jnp.where(el2 == v2, lanef, big), axis=-1, keepdims=True)
    t = jnp.exp(v2 - v1)
    w1 = gp / (1.0 + t)
    w2 = gp * t / (1.0 + t)
    sel1 = lanef == i1
    sel2 = lanef == i2
    oh = jnp.where(sel1 | sel2, 1.0, 0.0)
    r_i = lax.broadcasted_iota(I32, (tm, tm), 0)
    c_i = lax.broadcasted_iota(I32, (tm, tm), 1)
    lower = jnp.where(c_i < r_i, 1.0, 0.0).astype(BF16)
    before = _dot(lower, oh.astype(BF16)) + run_ref[...]
    rank1 = jnp.sum(jnp.where(sel1, before, 0.0), axis=-1, keepdims=True)
    rank2 = jnp.sum(jnp.where(sel2, before, 0.0), axis=-1, keepdims=True)
    run_ref[...] = run_ref[...] + jnp.sum(oh, axis=0, keepdims=True)
    cnt_ref[...] = run_ref[...]
    route = jnp.where(lane == 0, i1, jnp.where(lane == 1, i2, jnp.where(lane == 2, w1, jnp.where(
        lane == 3, w2, jnp.where(lane == 4, rank1, jnp.where(lane == 5, rank2, 0.0))))))
    route_ref[...] = route


def _expert_kernel(blk_e_ref, row_tok_ref, nxt_e_ref, nused_ref, h2_hbm, wg_hbm, wu_hbm, wd_hbm, ys_ref,
                   xbuf, wg_stg, wu_stg, wd_stg, wgb, wub, wdb, sem, wsem, *, layer):
    b = pl.program_id(0)
    nused = nused_ref[0]
    slot = lax.rem(b, 2)

    def gather_start(blk, sl):
        def body(r, carry):
            tok = row_tok_ref[blk * MOE_TM + r]
            pltpu.make_async_copy(h2_hbm.at[pl.ds(tok, 1), :], xbuf.at[sl, pl.ds(r, 1), :], sem.at[sl]).start()
            return carry

        lax.fori_loop(0, MOE_TM, body, 0, unroll=GATHER_UNROLL)

    def gather_wait(sl):
        pltpu.make_async_copy(h2_hbm.at[pl.ds(0, MOE_TM), :], xbuf.at[sl], sem.at[sl]).wait()

    def weight_copies(e):
        return (pltpu.make_async_copy(wg_hbm.at[layer, e], wg_stg, wsem.at[0]),
                pltpu.make_async_copy(wu_hbm.at[layer, e], wu_stg, wsem.at[1]),
                pltpu.make_async_copy(wd_hbm.at[layer, e], wd_stg, wsem.at[2]))

    @pl.when((b == 0) & (nused > 0))
    def _():
        gather_start(0, 0)
        for cp in weight_copies(blk_e_ref[0]):
            cp.start()

    @pl.when(b + 1 < nused)
    def _():
        gather_start(b + 1, 1 - slot)

    @pl.when(b < nused)
    def _():
        e = blk_e_ref[b]
        prev = blk_e_ref[jnp.maximum(b - 1, 0)]

        @pl.when((b == 0) | (e != prev))
        def _():
            for cp in weight_copies(e):
                cp.wait()
            wgb[...] = wg_stg[...].astype(BF16)
            wub[...] = wu_stg[...].astype(BF16)
            wdb[...] = wd_stg[...].astype(BF16)
            nxt = nxt_e_ref[e]

            @pl.when(nxt >= 0)
            def _():
                for cp in weight_copies(nxt):
                    cp.start()

        gather_wait(slot)
        x = xbuf[slot].astype(BF16)
        hid = _silu(_dot(x, wgb[...])) * _dot(x, wub[...])
        ys_ref[...] = _dot(hid.astype(BF16), wdb[...])

    @pl.when(b >= nused)
    def _():
        ys_ref[...] = jnp.zeros_like(ys_ref)


def _combine_kernel(d0_ref, d1_ref, ys_hbm, x1_ref, route_ref, g2_ref, o_ref, ybuf, sem):
    i = pl.program_id(0)
    n = pl.num_programs(0)
    slot = lax.rem(i, 2)

    def gather_start(tile, sl):
        def body(r, carry):
            t = tile * CMB_TT + r
            pltpu.make_async_copy(ys_hbm.at[pl.ds(d0_ref[t], 1), :], ybuf.at[sl, 0, pl.ds(r, 1), :], sem.at[sl]).start()
            pltpu.make_async_copy(ys_hbm.at[pl.ds(d1_ref[t], 1), :], ybuf.at[sl, 1, pl.ds(r, 1), :], sem.at[sl]).start()
            return carry

        lax.fori_loop(0, CMB_TT, body, 0, unroll=GATHER_UNROLL)

    def gather_wait(sl):
        for j in range(EXPERT_TOPK):
            pltpu.make_async_copy(ys_hbm.at[pl.ds(0, CMB_TT), :], ybuf.at[sl, j], sem.at[sl]).wait()

    @pl.when(i == 0)
    def _():
        gather_start(0, 0)

    @pl.when(i + 1 < n)
    def _():
        gather_start(i + 1, 1 - slot)

    gather_wait(slot)
    w0 = route_ref[:, 2:3]
    w1 = route_ref[:, 3:4]
    moe = ybuf[slot, 0] * w0 + ybuf[slot, 1] * w1
    o_ref[...] = x1_ref[...] + g2_ref[...] * moe


def _rope_tables(seq, dh):
    nf = dh // 4
    t = jnp.arange(seq)
    inv = ROPE_BASE ** (-jnp.arange(nf, dtype=F32) / nf)
    ang_r = (t // GRID_W).astype(F32)[:, None] * inv[None, :]
    ang_c = (t % GRID_W).astype(F32)[:, None] * inv[None, :]
    cos = jnp.concatenate([jnp.cos(ang_r)] * 2 + [jnp.cos(ang_c)] * 2, axis=-1)
    sin = jnp.concatenate([-jnp.sin(ang_r), jnp.sin(ang_r), -jnp.sin(ang_c), jnp.sin(ang_c)], axis=-1)
    return cos, sin


def kernel(x_prompt, x_sample, cache_attn_k, cache_attn_v, state_retention, c, c_ctx, w_mod, b_mod, norm1_g, norm2_g, w_in, conv_dw_w, conv_dw_b, conv_ln_g, conv_ln_b, ret_decay, q_norm_g, k_norm_g, attn_sink, w_out, router_group_w, router_group_b, router_expert_w, router_expert_b, expert_w_gate, expert_w_up, expert_w_down):
    batch, ctx_seq, d = x_prompt.shape
    dec_batch, lat_seq, _ = x_sample.shape
    depth = w_mod.shape[0]
    in_width = w_in.shape[2]
    conv_width, conv_ch = conv_dw_w.shape[1:]
    ret_heads = ret_decay.shape[2]
    ret_hd = state_retention.shape[-1]
    att_heads = attn_sink.shape[1]
    kv_heads, att_hd = cache_attn_k.shape[3:]
    past_len = cache_attn_k.shape[2]
    group = att_heads // kv_heads
    n_groups = router_group_w.shape[2]
    n_experts = router_expert_w.shape[2]
    d_expert = expert_w_gate.shape[3]
    nctx, nlat = batch * ctx_seq, dec_batch * lat_seq
    nt = nctx + nlat
    ret_w = ret_heads * ret_hd
    att_w = att_heads * att_hd
    kv_w = kv_heads * att_hd
    c_ca, c_cg = 0, conv_ch
    c_rq = 2 * conv_ch
    c_rk, c_rv, c_rg = c_rq + ret_w, c_rq + 2 * ret_w, c_rq + 3 * ret_w
    c_aq = c_rq + 4 * ret_w
    c_ak = c_aq + att_w
    c_av = c_ak + kv_w
    assert c_av + kv_w == in_width
    assert ctx_seq == CONV_TT == RET_TQ and lat_seq % ATT_TQ == 0 and INPROJ_TM == lat_seq
    assert conv_width // 2 <= CONV_HALO and n_experts + n_groups <= LANES

    mod_rows = SUBLANES
    assert 1 + dec_batch <= mod_rows
    cv = jnp.concatenate([c_ctx[None, :], c, jnp.zeros((mod_rows - 1 - dec_batch, d), F32)], axis=0)
    mods = _modulation(cv, w_mod, b_mod).reshape(depth, mod_rows, 1, 6 * d)

    def mod_row(t0):
        return jnp.where(t0 < nctx, 0, (t0 - nctx) // lat_seq + 1)

    def mod_spec(layer, chunk, tile):
        return pl.BlockSpec((None, None, 1, d), lambda i, *_: (layer, mod_row(i * tile), 0, chunk))

    lg = jax.nn.log_sigmoid(ret_decay.astype(F32)).reshape(-1)
    sinks = attn_sink.astype(F32).reshape(-1)
    cos_t, sin_t = _rope_tables(lat_seq, att_hd)
    rw = jnp.concatenate([router_expert_w, router_group_w, jnp.zeros((depth, d, LANES - n_experts - n_groups), F32)], axis=2)
    rb = jnp.concatenate([router_expert_b, router_group_b, jnp.zeros((depth, LANES - n_experts - n_groups), F32)], axis=1)
    rb = rb.reshape(depth, 1, LANES)
    kc_all = cache_attn_k.reshape(dec_batch, depth, past_len, kv_w)
    vc_all = cache_attn_v.reshape(dec_batch, depth, past_len, kv_w)
    n_assign = nt * EXPERT_TOPK
    p_rows = n_assign + n_experts * MOE_TM
    n_blk = p_rows // MOE_TM
    smem = pl.BlockSpec(memory_space=pltpu.SMEM)
    hbm = pl.BlockSpec(memory_space=pl.ANY)

    x = jnp.concatenate([x_prompt.reshape(nctx, d), x_sample.reshape(nlat, d)], axis=0)
    k_list, v_list, s_list = [], [], []
    for l in range(depth):
        n_j = in_width // INPROJ_TN
        proj = pl.pallas_call(
            _inproj_kernel,
            out_shape=jax.ShapeDtypeStruct((nt, in_width), BF16),
            grid=(nt // INPROJ_TM, n_j),
            in_specs=[
                pl.BlockSpec((INPROJ_TM, d), lambda i, j: (i, 0)),
                pl.BlockSpec((None, None, 1, d), lambda i, j, l=l: (l, mod_row(i * INPROJ_TM), 0, 1)),
                pl.BlockSpec((None, None, 1, d), lambda i, j, l=l: (l, mod_row(i * INPROJ_TM), 0, 0)),
                pl.BlockSpec((None, 1, d), lambda i, j, l=l: (l, 0, 0)),
                pl.BlockSpec((None, d, INPROJ_TN), lambda i, j, l=l: (l, 0, jnp.where(i == 0, j, n_j - 1))),
            ],
            out_specs=pl.BlockSpec((INPROJ_TM, INPROJ_TN), lambda i, j: (i, j)),
            scratch_shapes=[pltpu.VMEM((INPROJ_TM, d), BF16), pltpu.VMEM((n_j, d, INPROJ_TN), BF16)],
            compiler_params=_cparams(("arbitrary", "arbitrary")),
            name="inproj",
        )(x, mods, mods, norm1_g.reshape(depth, 1, d), w_in)

        hb = CONV_TT // CONV_HALO
        n_hb = nt // CONV_HALO
        conv_out = pl.pallas_call(
            functools.partial(_conv_kernel, nctx=nctx, ctx_seq=ctx_seq, lat_seq=lat_seq, width=conv_width),
            out_shape=jax.ShapeDtypeStruct((nt, conv_ch), BF16),
            grid=(nt // CONV_TT,),
            in_specs=[
                pl.BlockSpec((CONV_TT, conv_ch), lambda i: (i, c_ca // conv_ch)),
                pl.BlockSpec((CONV_TT, conv_ch), lambda i: (i, c_cg // conv_ch)),
                pl.BlockSpec((CONV_HALO, conv_ch), lambda i: (jnp.maximum(i * hb - 1, 0), c_ca // conv_ch)),
                pl.BlockSpec((CONV_HALO, conv_ch), lambda i: (jnp.maximum(i * hb - 1, 0), c_cg // conv_ch)),
                pl.BlockSpec((CONV_HALO, conv_ch), lambda i: (jnp.minimum((i + 1) * hb, n_hb - 1), c_ca // conv_ch)),
                pl.BlockSpec((CONV_HALO, conv_ch), lambda i: (jnp.minimum((i + 1) * hb, n_hb - 1), c_cg // conv_ch)),
                pl.BlockSpec((None, conv_width, conv_ch), lambda i, l=l: (l, 0, 0)),
                pl.BlockSpec((None, 1, conv_ch), lambda i, l=l: (l, 0, 0)),
                pl.BlockSpec((None, 1, conv_ch), lambda i, l=l: (l, 0, 0)),
                pl.BlockSpec((None, 1, conv_ch), lambda i, l=l: (l, 0, 0)),
            ],
            out_specs=pl.BlockSpec((CONV_TT, conv_ch), lambda i: (i, 0)),
            scratch_shapes=[pltpu.VMEM((SUBLANES, CONV_TT + 2 * CONV_HALO, conv_ch), F32)],
            compiler_params=_cparams(("arbitrary",)),
            name="conv",
        )(proj, proj, proj, proj, proj, proj, conv_dw_w, conv_dw_b.reshape(depth, 1, conv_ch),
          conv_ln_g.reshape(depth, 1, conv_ch), conv_ln_b.reshape(depth, 1, conv_ch))

        def ret_call(nseq, seq, row0, has_s0, emit_state):
            nq = seq // RET_TQ
            in_specs = [
                smem,
                pl.BlockSpec((RET_TQ, ret_hd), lambda b, h, qi: (row0 // RET_TQ + b * nq + qi, c_rq // ret_hd + h)),
                pl.BlockSpec((seq, ret_hd), lambda b, h, qi: (row0 // seq + b, c_rk // ret_hd + h)),
                pl.BlockSpec((seq, ret_hd), lambda b, h, qi: (row0 // seq + b, c_rv // ret_hd + h)),
                pl.BlockSpec((RET_TQ, ret_hd), lambda b, h, qi: (row0 // RET_TQ + b * nq + qi, c_rg // ret_hd + h)),
            ]
            args = [lg, proj, proj, proj, proj]
            if has_s0:
                in_specs.append(pl.BlockSpec((None, None, 2, None, ret_hd, ret_hd), lambda b, h, qi, l=l: (b, l, 0, h, 0, 0)))
                args.append(state_retention)
            out_shape = [jax.ShapeDtypeStruct((nseq * seq, ret_w), BF16)]
            out_specs = [pl.BlockSpec((RET_TQ, ret_hd), lambda b, h, qi: (b * nq + qi, h))]
            if emit_state:
                out_shape.append(jax.ShapeDtypeStruct((nseq, 2, ret_heads, ret_hd, ret_hd), F32))
                out_specs.append(pl.BlockSpec((None, 2, None, ret_hd, ret_hd), lambda b, h, qi: (b, 0, h, 0, 0)))
            return pl.pallas_call(
                functools.partial(_ret_kernel, layer=l, heads=ret_heads, seq=seq, has_s0=has_s0, emit_state=emit_state),
                out_shape=out_shape,
                grid=(nseq, ret_heads, nq),
                in_specs=in_specs,
                out_specs=out_specs,
                compiler_params=_cparams(("arbitrary", "arbitrary", "arbitrary")),
                name="retention",
            )(*args)

        ret_ctx, s_l = ret_call(batch, ctx_seq, 0, False, True)
        (ret_lat,) = ret_call(dec_batch, lat_seq, nctx, True, False)
        s_list.append(s_l)

        gq = q_norm_g.reshape(depth, 1, att_hd)
        gk = k_norm_g.reshape(depth, 1, att_hd)
        qw = group * att_hd
        att_ctx, k_l, v_l = pl.pallas_call(
            functools.partial(_attn_ctx_kernel, layer=l, group=group, scale=att_hd ** -0.5),
            out_shape=[jax.ShapeDtypeStruct((nctx, att_w), BF16),
                       jax.ShapeDtypeStruct((batch, ctx_seq, kv_w), F32),
                       jax.ShapeDtypeStruct((batch, ctx_seq, kv_w), F32)],
            grid=(batch, kv_heads),
            in_specs=[
                smem,
                pl.BlockSpec((ctx_seq, qw), lambda b, kv: (b, c_aq // qw + kv)),
                pl.BlockSpec((ctx_seq, att_hd), lambda b, kv: (b, c_ak // att_hd + kv)),
                pl.BlockSpec((ctx_seq, att_hd), lambda b, kv: (b, c_av // att_hd + kv)),
                pl.BlockSpec((None, 1, att_hd), lambda b, kv, l=l: (l, 0, 0)),
                pl.BlockSpec((None, 1, att_hd), lambda b, kv, l=l: (l, 0, 0)),
            ],
            out_specs=[
                pl.BlockSpec((ctx_seq, qw), lambda b, kv: (b, kv)),
                pl.BlockSpec((None, ctx_seq, att_hd), lambda b, kv: (b, 0, kv)),
                pl.BlockSpec((None, ctx_seq, att_hd), lambda b, kv: (b, 0, kv)),
            ],
            compiler_params=_cparams(("arbitrary", "arbitrary")),
            name="attn_ctx",
        )(sinks, proj, proj, proj, gq, gk)
        k_list.append(k_l)
        v_list.append(v_l)

        nq = lat_seq // ATT_TQ
        att_lat = pl.pallas_call(
            functools.partial(_attn_lat_kernel, layer=l, group=group, scale=att_hd ** -0.5, seq=lat_seq, nkv=kv_heads),
            out_shape=jax.ShapeDtypeStruct((nlat, att_w), BF16),
            grid=(dec_batch, kv_heads, nq),
            in_specs=[
                smem,
                pl.BlockSpec((ATT_TQ, qw), lambda b, kv, qi: (nctx // ATT_TQ + b * nq + qi, c_aq // qw + kv)),
                pl.BlockSpec((lat_seq, att_hd), lambda b, kv, qi: (nctx // lat_seq + b, c_ak // att_hd + kv)),
                pl.BlockSpec((lat_seq, att_hd), lambda b, kv, qi: (nctx // lat_seq + b, c_av // att_hd + kv)),
                pl.BlockSpec((None, None, past_len, att_hd), lambda b, kv, qi, l=l: (b, l, 0, kv)),
                pl.BlockSpec((None, None, past_len, att_hd), lambda b, kv, qi, l=l: (b, l, 0, kv)),
                pl.BlockSpec((None, 1, att_hd), lambda b, kv, qi, l=l: (l, 0, 0)),
                pl.BlockSpec((None, 1, att_hd), lambda b, kv, qi, l=l: (l, 0, 0)),
                pl.BlockSpec((lat_seq, att_hd), lambda b, kv, qi: (0, 0)),
                pl.BlockSpec((lat_seq, att_hd), lambda b, kv, qi: (0, 0)),
            ],
            out_specs=pl.BlockSpec((ATT_TQ, qw), lambda b, kv, qi: (b * nq + qi, kv)),
            scratch_shapes=[pltpu.VMEM((lat_seq, att_hd), BF16)],
            compiler_params=_cparams(("arbitrary", "arbitrary", "arbitrary")),
            name="attn_lat",
        )(sinks, proj, proj, proj, kc_all, vc_all, gq, gk, cos_t, sin_t)

        ctx_tiles = nctx // OUT_TM
        lat_tiles = nlat // OUT_TM

        def ctx_rows(i):
            return jnp.minimum(i, ctx_tiles - 1)

        def lat_rows(i):
            return jnp.maximum(i - ctx_tiles, 0)

        x1, h2, route, cnt = pl.pallas_call(
            functools.partial(_outproj_kernel, n_groups=n_groups, n_experts=n_experts, ctx_tiles=ctx_tiles),
            out_shape=[jax.ShapeDtypeStruct((nt, d), F32), jax.ShapeDtypeStruct((nt, d), F32),
                       jax.ShapeDtypeStruct((nt, LANES), F32), jax.ShapeDtypeStruct((1, LANES), F32)],
            grid=(ctx_tiles + lat_tiles,),
            in_specs=[
                pl.BlockSpec((OUT_TM, conv_ch), lambda i: (i, 0)),
                pl.BlockSpec((OUT_TM, ret_w), lambda i: (ctx_rows(i), 0)),
                pl.BlockSpec((OUT_TM, ret_w), lambda i: (lat_rows(i), 0)),
                pl.BlockSpec((OUT_TM, att_w), lambda i: (ctx_rows(i), 0)),
                pl.BlockSpec((OUT_TM, att_w), lambda i: (lat_rows(i), 0)),
                pl.BlockSpec((OUT_TM, d), lambda i: (i, 0)),
                mod_spec(l, 2, OUT_TM),
                mod_spec(l, 4, OUT_TM),
                mod_spec(l, 3, OUT_TM),
                pl.BlockSpec((None, 1, d), lambda i, l=l: (l, 0, 0)),
                pl.BlockSpec((None, d, d), lambda i, l=l: (l, 0, 0), pipeline_mode=pl.Buffered(1)),
                pl.BlockSpec((None, d, LANES), lambda i, l=l: (l, 0, 0), pipeline_mode=pl.Buffered(1)),
                pl.BlockSpec((None, 1, LANES), lambda i, l=l: (l, 0, 0)),
            ],
            out_specs=[
                pl.BlockSpec((OUT_TM, d), lambda i: (i, 0)),
                pl.BlockSpec((OUT_TM, d), lambda i: (i, 0)),
                pl.BlockSpec((OUT_TM, LANES), lambda i: (i, 0)),
                pl.BlockSpec((1, LANES), lambda i: (0, 0)),
            ],
            scratch_shapes=[pltpu.VMEM((d, d), BF16), pltpu.VMEM((d, LANES), BF16), pltpu.VMEM((d, LANES), BF16),
                            pltpu.VMEM((1, LANES), F32)],
            compiler_params=_cparams(("arbitrary",)),
            name="outproj_router",
        )(conv_out, ret_ctx, ret_lat, att_ctx, att_lat, x, mods, mods, mods, norm2_g.reshape(depth, 1, d), w_out, rw, rb)

        eid = route[:, 0:EXPERT_TOPK].astype(I32)
        rank = route[:, 4:4 + EXPERT_TOPK].astype(I32)
        counts = cnt[0, :n_experts].astype(I32)
        padded = ((counts + MOE_TM - 1) // MOE_TM) * MOE_TM
        pends = jnp.cumsum(padded)
        pstarts = pends - padded
        ek = jnp.arange(n_experts, dtype=I32)
        dest = rank + jnp.sum(jnp.where(eid[:, :, None] == ek, pstarts, 0), axis=-1)
        tok = jnp.broadcast_to(jnp.arange(nt, dtype=I32)[:, None], (nt, EXPERT_TOPK))
        row_tok = jnp.zeros((p_rows,), I32).at[dest.reshape(-1)].set(
            tok.reshape(-1), unique_indices=True, mode='promise_in_bounds')
        blk_start = jnp.arange(n_blk, dtype=I32) * MOE_TM
        blk_e = jnp.minimum(jnp.sum((pends[None, :] <= blk_start[:, None]).astype(I32), axis=1), n_experts - 1)
        later = (ek[None, :] > ek[:, None]) & (counts[None, :] > 0)
        nxt_e = jnp.min(jnp.where(later, ek[None, :], n_experts), axis=1)
        nxt_e = jnp.where(nxt_e == n_experts, -1, nxt_e).astype(I32)
        nused = (pends[-1:] // MOE_TM).astype(I32)

        ys = pl.pallas_call(
            functools.partial(_expert_kernel, layer=l),
            out_shape=jax.ShapeDtypeStruct((p_rows, d), F32),
            grid_spec=pltpu.PrefetchScalarGridSpec(
                num_scalar_prefetch=4,
                grid=(n_blk,),
                in_specs=[hbm, hbm, hbm, hbm],
                out_specs=pl.BlockSpec((MOE_TM, d), lambda b, be, rt, nx, nu: (b, 0)),
                scratch_shapes=[pltpu.VMEM((2, MOE_TM, d), F32),
                                pltpu.VMEM((d, d_expert), F32), pltpu.VMEM((d, d_expert), F32),
                                pltpu.VMEM((d_expert, d), F32),
                                pltpu.VMEM((d, d_expert), BF16), pltpu.VMEM((d, d_expert), BF16),
                                pltpu.VMEM((d_expert, d), BF16),
                                pltpu.SemaphoreType.DMA((2,)), pltpu.SemaphoreType.DMA((3,))],
            ),
            compiler_params=_cparams(("arbitrary",)),
            name="experts",
        )(blk_e, row_tok, nxt_e, nused, h2, expert_w_gate, expert_w_up, expert_w_down)

        x = pl.pallas_call(
            _combine_kernel,
            out_shape=jax.ShapeDtypeStruct((nt, d), F32),
            grid_spec=pltpu.PrefetchScalarGridSpec(
                num_scalar_prefetch=2,
                grid=(nt // CMB_TT,),
                in_specs=[
                    hbm,
                    pl.BlockSpec((CMB_TT, d), lambda i, d0, d1: (i, 0)),
                    pl.BlockSpec((CMB_TT, LANES), lambda i, d0, d1: (i, 0)),
                    pl.BlockSpec((None, None, 1, d), lambda i, d0, d1, l=l: (l, mod_row(i * CMB_TT), 0, 5)),
                ],
                out_specs=pl.BlockSpec((CMB_TT, d), lambda i, d0, d1: (i, 0)),
                scratch_shapes=[pltpu.VMEM((2, EXPERT_TOPK, CMB_TT, d), F32), pltpu.SemaphoreType.DMA((2,))],
            ),
            compiler_params=_cparams(("arbitrary",)),
            name="combine",
        )(dest[:, 0], dest[:, 1], ys, x1, route, mods)

    y_prompt = x[:nctx].reshape(batch, ctx_seq, d)
    y_sample = x[nctx:].reshape(dec_batch, lat_seq, d)
    new_k = jnp.stack(k_list, axis=1).reshape(batch, depth, ctx_seq, kv_heads, att_hd)
    new_v = jnp.stack(v_list, axis=1).reshape(batch, depth, ctx_seq, kv_heads, att_hd)
    new_s = jnp.stack(s_list, axis=1)
    return (y_prompt, y_sample, new_k, new_v, new_s)
```

```python
import functools

import jax
import jax.numpy as jnp
from jax import lax
from jax.experimental import pallas as pl
from jax.experimental.pallas import tpu as pltpu

F32, BF16, I32 = jnp.float32, jnp.bfloat16, jnp.int32

GRID_W = 64
ATT_WINDOW = 128
ROPE_BASE = 10000.0
EXPERT_TOPK = 2
EPS = 1e-6
NEG_INF = -1e30

LANES = 128
SUBLANES = 8
VMEM_LIMIT_BYTES = 56 * 1024 * 1024

MOD_TN = 1024
INPROJ_TM = 1024
INPROJ_TN = 512
CONV_TT = 256
CONV_HALO = 16
CONV_RC = 32
RET_TQ = 256
ATT_TQ = 256
OUT_TM = 256
MOE_TM = 128
CMB_TT = 256
GATHER_UNROLL = 8


def _cparams(sem):
    return pltpu.CompilerParams(dimension_semantics=sem, vmem_limit_bytes=VMEM_LIMIT_BYTES)


def _rms(x, g):
    return x * lax.rsqrt(jnp.mean(x * x, axis=-1, keepdims=True) + EPS) * g


def _silu(x):
    return x * jax.nn.sigmoid(x)


def _dot(a, b):
    return jnp.dot(a, b, preferred_element_type=F32)


def _dot_nt(a, b):
    return lax.dot_general(a, b, (((1,), (1,)), ((), ())), preferred_element_type=F32)


def _dot_tn(a, b):
    return lax.dot_general(a, b, (((0,), (0,)), ((), ())), preferred_element_type=F32)


def _mod_kernel(cv_ref, w_ref, b_ref, o_ref):
    s = _silu(cv_ref[...]).astype(BF16)
    o_ref[...] = _dot(s, w_ref[...].astype(BF16)) + b_ref[...]


def _modulation(cv, w_mod, b_mod):
    depth, d, n6 = w_mod.shape
    rows = cv.shape[0]
    return pl.pallas_call(
        _mod_kernel,
        out_shape=jax.ShapeDtypeStruct((depth, rows, n6), F32),
        grid=(depth, n6 // MOD_TN),
        in_specs=[
            pl.BlockSpec((rows, d), lambda l, j: (0, 0)),
            pl.BlockSpec((None, d, MOD_TN), lambda l, j: (l, 0, j)),
            pl.BlockSpec((None, 1, MOD_TN), lambda l, j: (l, 0, j)),
        ],
        out_specs=pl.BlockSpec((None, rows, MOD_TN), lambda l, j: (l, 0, j)),
        compiler_params=_cparams(("arbitrary", "arbitrary")),
        name="modulation",
    )(cv, w_mod, b_mod.reshape(depth, 1, n6))


def _inproj_kernel(x_ref, sc_ref, sh_ref, g_ref, w_ref, o_ref, h_ref, wb_ref):
    i = pl.program_id(0)
    j = pl.program_id(1)

    @pl.when(i == 0)
    def _():
        wb_ref[j] = w_ref[...].astype(BF16)

    @pl.when(j == 0)
    def _():
        h = _rms(x_ref[...], g_ref[...]) * (1.0 + sc_ref[...]) + sh_ref[...]
        h_ref[...] = h.astype(BF16)

    o_ref[...] = _dot(h_ref[...], wb_ref[j]).astype(BF16)


def _conv_kernel(a_ref, g_ref, ap_ref, gp_ref, an_ref, gn_ref, w_ref, b_ref, lng_ref, lnb_ref, o_ref,
                 hs_ref, *, nctx, ctx_seq, lat_seq, width):
    t0 = pl.program_id(0) * CONV_TT
    is_ctx = t0 < nctx
    seq_len = jnp.where(is_ctx, ctx_seq, lat_seq)
    pos = lax.rem(jnp.where(is_ctx, t0, t0 - nctx), seq_len)
    first = pos == 0
    last = pos + CONV_TT == seq_len
    rows = CONV_TT + 2 * CONV_HALO
    span = rows - SUBLANES
    def glu(a, g):
        return a[...].astype(F32) * jax.nn.sigmoid(g[...].astype(F32))

    hs_ref[0, 0:CONV_HALO, :] = jnp.where(first, 0.0, glu(ap_ref, gp_ref))
    hs_ref[0, CONV_HALO:CONV_HALO + CONV_TT, :] = glu(a_ref, g_ref)
    hs_ref[0, CONV_HALO + CONV_TT:rows, :] = jnp.where(last, 0.0, glu(an_ref, gn_ref))
    for s in range(1, SUBLANES):
        hs_ref[s, 0:span, :] = hs_ref[0, s:s + span, :]
    base = CONV_HALO - width // 2
    for r0 in range(0, CONV_TT, CONV_RC):
        acc = jnp.zeros((CONV_RC, o_ref.shape[1]), F32)
        for k in range(width):
            idx = k + base
            s, q = idx % SUBLANES, idx // SUBLANES
            acc = acc + w_ref[k:k + 1, :] * hs_ref[s, r0 + SUBLANES * q:r0 + SUBLANES * q + CONV_RC, :]
        hf = acc + b_ref[...]
        mu = jnp.mean(hf, axis=-1, keepdims=True)
        var = jnp.mean(jnp.square(hf - mu), axis=-1, keepdims=True)
        y = (hf - mu) * lax.rsqrt(var + EPS) * lng_ref[...] + lnb_ref[...]
        o_ref[r0:r0 + CONV_RC, :] = _silu(y).astype(BF16)


def _ret_kernel(lg_ref, q_ref, k_ref, v_ref, gate_ref, *rest, layer, heads, seq, has_s0, emit_state):
    rest = list(rest)
    s0_ref = rest.pop(0) if has_s0 else None
    o_ref = rest.pop(0)
    st_ref = rest.pop(0) if emit_state else None
    h = pl.program_id(1)
    qi = pl.program_id(2)
    lgf = lg_ref[layer * 2 * heads + h]
    lgb = lg_ref[layer * 2 * heads + heads + h]
    dh = q_ref.shape[1]
    q = q_ref[...].astype(F32)
    k = k_ref[...].astype(F32) * (dh ** -0.5)
    vb = v_ref[...]
    a = _dot_nt(q_ref[...], k.astype(BF16))
    n_idx = qi * RET_TQ + lax.broadcasted_iota(I32, a.shape, 0)
    m_idx = lax.broadcasted_iota(I32, a.shape, 1)
    rel = (n_idx - m_idx).astype(F32)
    decay = jnp.exp(jnp.where(rel >= 0.0, lgf, -lgb) * rel)
    o = _dot((a * decay).astype(BF16), vb)
    if has_s0:
        npos = (qi * RET_TQ + lax.broadcasted_iota(I32, (RET_TQ, 1), 0)).astype(F32)
        cf = jnp.exp(lgf * (npos + 1.0))
        cb = jnp.exp(lgb * (seq - npos))
        o = o + _dot((q * cf).astype(BF16), s0_ref[0].astype(BF16)) + _dot((q * cb).astype(BF16), s0_ref[1].astype(BF16))
    y = o * lax.rsqrt(jnp.mean(o * o, axis=-1, keepdims=True) + EPS)
    o_ref[...] = (_silu(gate_ref[...].astype(F32)) * y).astype(BF16)
    if emit_state:
        m = lax.broadcasted_iota(I32, (seq, 1), 0).astype(F32)
        st_ref[0] = _dot_tn((k * jnp.exp(lgf * (seq - 1.0 - m))).astype(BF16), vb)
        st_ref[1] = _dot_tn((k * jnp.exp(lgb * m)).astype(BF16), vb)


def _attn_ctx_kernel(sink_ref, q_ref, k_ref, v_ref, gq_ref, gk_ref, o_ref, ko_ref, vo_ref, *, layer, group, scale):
    kv = pl.program_id(1)
    dh = k_ref.shape[1]
    kn = _rms(k_ref[...].astype(F32), gk_ref[...])
    vb = v_ref[...]
    ko_ref[...] = kn
    vo_ref[...] = vb.astype(F32)
    knb = kn.astype(BF16)
    for g in range(group):
        sink = sink_ref[layer * pl.num_programs(1) * group + kv * group + g]
        qn = _rms(q_ref[:, g * dh:(g + 1) * dh].astype(F32), gq_ref[...])
        s = _dot_nt(qn.astype(BF16), knb) * scale
        m = jnp.maximum(jnp.max(s, axis=-1, keepdims=True), sink)
        p = jnp.exp(s - m)
        den = jnp.sum(p, axis=-1, keepdims=True) + jnp.exp(sink - m)
        o_ref[:, g * dh:(g + 1) * dh] = (_dot(p.astype(BF16), vb) / den).astype(BF16)


def _rope(x, cos, sin_signed):
    lane = lax.broadcasted_iota(I32, x.shape, 1)
    half = x.shape[1] // 4
    swapped = jnp.where(lax.rem(lane, 2 * half) < half, pltpu.roll(x, x.shape[1] - half, 1), pltpu.roll(x, half, 1))
    return x * cos + swapped * sin_signed


def _attn_lat_kernel(sink_ref, q_ref, k_ref, v_ref, kc_ref, vc_ref, gq_ref, gk_ref, cos_ref, sin_ref,
                     o_ref, kr_ref, *, layer, group, scale, seq, nkv):
    kv = pl.program_id(1)
    qi = pl.program_id(2)
    dh = k_ref.shape[1]
    win = ATT_TQ + 2 * ATT_WINDOW

    @pl.when(qi == 0)
    def _():
        kr_ref[...] = _rope(_rms(k_ref[...].astype(F32), gk_ref[...]), cos_ref[...], sin_ref[...]).astype(BF16)

    q0 = pl.multiple_of(qi * ATT_TQ, ATT_TQ)
    start = pl.multiple_of(jnp.clip(qi * ATT_TQ - ATT_WINDOW, 0, seq - win), ATT_WINDOW)
    kw = kr_ref[pl.ds(start, win), :]
    vw = v_ref[pl.ds(start, win), :]
    kcb = kc_ref[...].astype(BF16)
    vcb = vc_ref[...].astype(BF16)
    qpos = q0 + lax.broadcasted_iota(I32, (ATT_TQ, win), 0)
    kpos = start + lax.broadcasted_iota(I32, (ATT_TQ, win), 1)
    valid = jnp.abs(qpos - kpos) <= ATT_WINDOW
    cs = cos_ref[pl.ds(q0, ATT_TQ), :]
    sn = sin_ref[pl.ds(q0, ATT_TQ), :]
    for g in range(group):
        sink = sink_ref[layer * nkv * group + kv * group + g]
        qr = _rope(_rms(q_ref[:, g * dh:(g + 1) * dh].astype(F32), gq_ref[...]), cs, sn).astype(BF16)
        s_loc = jnp.where(valid, _dot_nt(qr, kw) * scale, NEG_INF)
        s_ctx = _dot_nt(qr, kcb) * scale
        m = jnp.maximum(jnp.maximum(jnp.max(s_loc, axis=-1, keepdims=True), jnp.max(s_ctx, axis=-1, keepdims=True)), sink)
        p_loc = jnp.exp(s_loc - m)
        p_ctx = jnp.exp(s_ctx - m)
        den = jnp.sum(p_loc, axis=-1, keepdims=True) + jnp.sum(p_ctx, axis=-1, keepdims=True) + jnp.exp(sink - m)
        o = _dot(p_loc.astype(BF16), vw) + _dot(p_ctx.astype(BF16), vcb)
        o_ref[:, g * dh:(g + 1) * dh] = (o / den).astype(BF16)


def _split_bf16(x):
    hi = x.astype(BF16)
    lo = (x - hi.astype(F32)).astype(BF16)
    return hi, lo


def _outproj_kernel(conv_ref, retc_ref, retl_ref, attc_ref, attl_ref, x_ref, g1_ref, sc2_ref, sh2_ref, n2_ref, wout_ref,
                    rw_ref, rb_ref, x1_ref, h2_ref, route_ref, cnt_ref, wb_ref, rwh_ref, rwl_ref, run_ref,
                    *, n_groups, n_experts, ctx_tiles):
    i = pl.program_id(0)
    c_conv = conv_ref.shape[1]
    c_ret = retc_ref.shape[1]
    is_ctx = i < ctx_tiles
    ret = jnp.where(is_ctx, retc_ref[...], retl_ref[...])
    att = jnp.where(is_ctx, attc_ref[...], attl_ref[...])

    @pl.when(i == 0)
    def _():
        wb_ref[...] = wout_ref[...].astype(BF16)
        hi, lo = _split_bf16(rw_ref[...])
        rwh_ref[...] = hi
        rwl_ref[...] = lo
        run_ref[...] = jnp.zeros_like(run_ref)

    mix = (_dot(conv_ref[...], wb_ref[0:c_conv, :]) + _dot(ret, wb_ref[c_conv:c_conv + c_ret, :])
           + _dot(att, wb_ref[c_conv + c_ret:, :]))
    x1 = x_ref[...] + g1_ref[...] * mix
    x1_ref[...] = x1
    h2 = _rms(x1, n2_ref[...]) * (1.0 + sc2_ref[...]) + sh2_ref[...]
    _pack_rows(h2, h2_ref)
    hh, hl = _split_bf16(h2)
    lg = _dot(hh, rwh_ref[...]) + _dot(hl, rwh_ref[...]) + _dot(hh, rwl_ref[...]) + rb_ref[...]
    tm = lg.shape[0]
    lane = lax.broadcasted_iota(I32, lg.shape, 1)
    lanef = lane.astype(F32)
    big = float(LANES)
    gmask = (lane >= n_experts) & (lane < n_experts + n_groups)
    gl = jnp.where(gmask, lg, NEG_INF)
    gmax = jnp.max(gl, axis=-1, keepdims=True)
    gidx = jnp.min(jnp.where(gl == gmax, lanef, big), axis=-1, keepdims=True) - float(n_experts)
    gp = 1.0 / jnp.sum(jnp.where(gmask, jnp.exp(gl - gmax), 0.0), axis=-1, keepdims=True)
    per_group = n_experts // n_groups
    emask = (lanef >= gidx * per_group) & (lanef < (gidx + 1.0) * per_group)
    el = jnp.where(emask, lg, NEG_INF)
    v1 = jnp.max(el, axis=-1, keepdims=True)
    i1 = jnp.min(jnp.where(el == v1, lanef, big), axis=-1, keepdims=True)
    el2 = jnp.where(lanef == i1, NEG_INF, el)
    v2 = jnp.max(el2, axis=-1, keepdims=True)
    i2 = jnp.min(jnp.where(el2 == v2, lanef, big), axis=-1, keepdims=True)
    t = jnp.exp(v2 - v1)
    w1 = gp / (1.0 + t)
    w2 = gp * t / (1.0 + t)
    sel1 = lanef == i1
    sel2 = lanef == i2
    oh = jnp.where(sel1 | sel2, 1.0, 0.0)
    r_i = lax.broadcasted_iota(I32, (tm, tm), 0)
    c_i = lax.broadcasted_iota(I32, (tm, tm), 1)
    lower = jnp.where(c_i < r_i, 1.0, 0.0).astype(BF16)
    before = _dot(lower, oh.astype(BF16)) + run_ref[...]
    rank1 = jnp.sum(jnp.where(sel1, before, 0.0), axis=-1, keepdims=True)
    rank2 = jnp.sum(jnp.where(sel2, before, 0.0), axis=-1, keepdims=True)
    run_ref[...] = run_ref[...] + jnp.sum(oh, axis=0, keepdims=True)
    cnt_ref[...] = run_ref[...]
    route = jnp.where(lane == 0, i1, jnp.where(lane == 1, i2, jnp.where(lane == 2, w1, jnp.where(
        lane == 3, w2, jnp.where(lane == 4, rank1, jnp.where(lane == 5, rank2, 0.0))))))
    route_ref[...] = route


PACK_SUB = SUBLANES


def _pack_rows(v, o_ref):
    tm = v.shape[0]
    half = PACK_SUB * LANES
    bits = pltpu.bitcast(v.astype(BF16).astype(F32), jnp.uint32)
    for c in range(PACK_SUB):
        lo = lax.shift_right_logical(bits[:, c * LANES:(c + 1) * LANES], jnp.uint32(16))
        hi = bits[:, half + c * LANES:half + (c + 1) * LANES] & jnp.uint32(0xFFFF0000)
        o_ref[pl.ds(c, tm, stride=PACK_SUB), :] = lo | hi


def _unpack_chunk(src_ref, c, rows):
    w = src_ref[pl.ds(c, rows, stride=PACK_SUB), :]
    lo = pltpu.bitcast(lax.shift_left(w, jnp.uint32(16)), F32)
    hi = pltpu.bitcast(w & jnp.uint32(0xFFFF0000), F32)
    return lo, hi


def _expert_kernel(blk_e_ref, row_tok_ref, nxt_e_ref, nused_ref, h2_hbm, wg_hbm, wu_hbm, wd_hbm, ys_ref,
                   xbuf, xb, wg_stg, wu_stg, wd_stg, wgb, wub, wdb, sem, wsem, *, layer):
    b = pl.program_id(0)
    nused = nused_ref[0]
    slot = lax.rem(b, 2)

    def gather_start(blk, sl):
        def body(r, carry):
            src = pl.multiple_of(row_tok_ref[blk * MOE_TM + r] * PACK_SUB, PACK_SUB)
            dst = pl.multiple_of(r * PACK_SUB, PACK_SUB)
            pltpu.make_async_copy(h2_hbm.at[pl.ds(src, PACK_SUB), :], xbuf.at[sl, pl.ds(dst, PACK_SUB), :],
                                  sem.at[sl]).start()
            return carry

        lax.fori_loop(0, MOE_TM, body, 0, unroll=GATHER_UNROLL)

    def gather_wait(sl):
        pltpu.make_async_copy(h2_hbm.at[pl.ds(0, MOE_TM * PACK_SUB), :], xbuf.at[sl], sem.at[sl]).wait()

    def weight_copies(e):
        return (pltpu.make_async_copy(wg_hbm.at[layer, e], wg_stg, wsem.at[0]),
                pltpu.make_async_copy(wu_hbm.at[layer, e], wu_stg, wsem.at[1]),
                pltpu.make_async_copy(wd_hbm.at[layer, e], wd_stg, wsem.at[2]))

    @pl.when((b == 0) & (nused > 0))
    def _():
        gather_start(0, 0)
        for cp in weight_copies(blk_e_ref[0]):
            cp.start()

    @pl.when(b + 1 < nused)
    def _():
        gather_start(b + 1, 1 - slot)

    @pl.when(b < nused)
    def _():
        e = blk_e_ref[b]
        prev = blk_e_ref[jnp.maximum(b - 1, 0)]

        @pl.when((b == 0) | (e != prev))
        def _():
            for cp in weight_copies(e):
                cp.wait()
            wgb[...] = wg_stg[...].astype(BF16)
            wub[...] = wu_stg[...].astype(BF16)
            wdb[...] = wd_stg[...].astype(BF16)
            nxt = nxt_e_ref[e]

            @pl.when(nxt >= 0)
            def _():
                for cp in weight_copies(nxt):
                    cp.start()

        gather_wait(slot)
        half = PACK_SUB * LANES
        for c in range(PACK_SUB):
            lo, hi = _unpack_chunk(xbuf.at[slot], c, MOE_TM)
            xb[:, c * LANES:(c + 1) * LANES] = lo.astype(BF16)
            xb[:, half + c * LANES:half + (c + 1) * LANES] = hi.astype(BF16)
        x = xb[...]
        hid = _silu(_dot(x, wgb[...])) * _dot(x, wub[...])
        _pack_rows(_dot(hid.astype(BF16), wdb[...]), ys_ref)

    @pl.when(b >= nused)
    def _():
        ys_ref[...] = jnp.zeros_like(ys_ref)


def _combine_kernel(d0_ref, d1_ref, ys_hbm, x1_ref, route_ref, g2_ref, o_ref, ybuf, sem):
    i = pl.program_id(0)
    n = pl.num_programs(0)
    slot = lax.rem(i, 2)

    def gather_start(tile, sl):
        def body(r, carry):
            t = tile * CMB_TT + r
            dst = pl.multiple_of(r * PACK_SUB, PACK_SUB)
            for j, d_ref in enumerate((d0_ref, d1_ref)):
                src = pl.multiple_of(d_ref[t] * PACK_SUB, PACK_SUB)
                pltpu.make_async_copy(ys_hbm.at[pl.ds(src, PACK_SUB), :], ybuf.at[sl, j, pl.ds(dst, PACK_SUB), :],
                                      sem.at[sl]).start()
            return carry

        lax.fori_loop(0, CMB_TT, body, 0, unroll=GATHER_UNROLL)

    def gather_wait(sl):
        for j in range(EXPERT_TOPK):
            pltpu.make_async_copy(ys_hbm.at[pl.ds(0, CMB_TT * PACK_SUB), :], ybuf.at[sl, j], sem.at[sl]).wait()

    @pl.when(i == 0)
    def _():
        gather_start(0, 0)

    @pl.when(i + 1 < n)
    def _():
        gather_start(i + 1, 1 - slot)

    gather_wait(slot)
    w0 = route_ref[:, 2:3]
    w1 = route_ref[:, 3:4]
    half = PACK_SUB * LANES
    for c in range(PACK_SUB):
        lo0, hi0 = _unpack_chunk(ybuf.at[slot, 0], c, CMB_TT)
        lo1, hi1 = _unpack_chunk(ybuf.at[slot, 1], c, CMB_TT)
        for off, y0, y1 in ((c * LANES, lo0, lo1), (half + c * LANES, hi0, hi1)):
            cols = slice(off, off + LANES)
            o_ref[:, cols] = x1_ref[:, cols] + g2_ref[:, cols] * (y0 * w0 + y1 * w1)


def _rope_tables(seq, dh):
    nf = dh // 4
    t = jnp.arange(seq)
    inv = ROPE_BASE ** (-jnp.arange(nf, dtype=F32) / nf)
    ang_r = (t // GRID_W).astype(F32)[:, None] * inv[None, :]
    ang_c = (t % GRID_W).astype(F32)[:, None] * inv[None, :]
    cos = jnp.concatenate([jnp.cos(ang_r)] * 2 + [jnp.cos(ang_c)] * 2, axis=-1)
    sin = jnp.concatenate([-jnp.sin(ang_r), jnp.sin(ang_r), -jnp.sin(ang_c), jnp.sin(ang_c)], axis=-1)
    return cos, sin


def kernel(x_prompt, x_sample, cache_attn_k, cache_attn_v, state_retention, c, c_ctx, w_mod, b_mod, norm1_g, norm2_g, w_in, conv_dw_w, conv_dw_b, conv_ln_g, conv_ln_b, ret_decay, q_norm_g, k_norm_g, attn_sink, w_out, router_group_w, router_group_b, router_expert_w, router_expert_b, expert_w_gate, expert_w_up, expert_w_down):
    batch, ctx_seq, d = x_prompt.shape
    dec_batch, lat_seq, _ = x_sample.shape
    depth = w_mod.shape[0]
    in_width = w_in.shape[2]
    conv_width, conv_ch = conv_dw_w.shape[1:]
    ret_heads = ret_decay.shape[2]
    ret_hd = state_retention.shape[-1]
    att_heads = attn_sink.shape[1]
    kv_heads, att_hd = cache_attn_k.shape[3:]
    past_len = cache_attn_k.shape[2]
    group = att_heads // kv_heads
    n_groups = router_group_w.shape[2]
    n_experts = router_expert_w.shape[2]
    d_expert = expert_w_gate.shape[3]
    nctx, nlat = batch * ctx_seq, dec_batch * lat_seq
    nt = nctx + nlat
    ret_w = ret_heads * ret_hd
    att_w = att_heads * att_hd
    kv_w = kv_heads * att_hd
    c_ca, c_cg = 0, conv_ch
    c_rq = 2 * conv_ch
    c_rk, c_rv, c_rg = c_rq + ret_w, c_rq + 2 * ret_w, c_rq + 3 * ret_w
    c_aq = c_rq + 4 * ret_w
    c_ak = c_aq + att_w
    c_av = c_ak + kv_w
    assert c_av + kv_w == in_width
    assert ctx_seq == CONV_TT == RET_TQ and lat_seq % ATT_TQ == 0 and INPROJ_TM == lat_seq
    assert conv_width // 2 <= CONV_HALO and n_experts + n_groups <= LANES
    assert d == 2 * PACK_SUB * LANES

    mod_rows = SUBLANES
    assert 1 + dec_batch <= mod_rows
    cv = jnp.concatenate([c_ctx[None, :], c, jnp.zeros((mod_rows - 1 - dec_batch, d), F32)], axis=0)
    mods = _modulation(cv, w_mod, b_mod).reshape(depth, mod_rows, 1, 6 * d)

    def mod_row(t0):
        return jnp.where(t0 < nctx, 0, (t0 - nctx) // lat_seq + 1)

    def mod_spec(layer, chunk, tile):
        return pl.BlockSpec((None, None, 1, d), lambda i, *_: (layer, mod_row(i * tile), 0, chunk))

    lg = jax.nn.log_sigmoid(ret_decay.astype(F32)).reshape(-1)
    sinks = attn_sink.astype(F32).reshape(-1)
    cos_t, sin_t = _rope_tables(lat_seq, att_hd)
    rw = jnp.concatenate([router_expert_w, router_group_w, jnp.zeros((depth, d, LANES - n_experts - n_groups), F32)], axis=2)
    rb = jnp.concatenate([router_expert_b, router_group_b, jnp.zeros((depth, LANES - n_experts - n_groups), F32)], axis=1)
    rb = rb.reshape(depth, 1, LANES)
    kc_all = cache_attn_k.reshape(dec_batch, depth, past_len, kv_w)
    vc_all = cache_attn_v.reshape(dec_batch, depth, past_len, kv_w)
    n_assign = nt * EXPERT_TOPK
    p_rows = n_assign + n_experts * MOE_TM
    n_blk = p_rows // MOE_TM
    smem = pl.BlockSpec(memory_space=pltpu.SMEM)
    hbm = pl.BlockSpec(memory_space=pl.ANY)

    x = jnp.concatenate([x_prompt.reshape(nctx, d), x_sample.reshape(nlat, d)], axis=0)
    k_list, v_list, s_list = [], [], []
    for l in range(depth):
        n_j = in_width // INPROJ_TN
        proj = pl.pallas_call(
            _inproj_kernel,
            out_shape=jax.ShapeDtypeStruct((nt, in_width), BF16),
            grid=(nt // INPROJ_TM, n_j),
            in_specs=[
                pl.BlockSpec((INPROJ_TM, d), lambda i, j: (i, 0)),
                pl.BlockSpec((None, None, 1, d), lambda i, j, l=l: (l, mod_row(i * INPROJ_TM), 0, 1)),
                pl.BlockSpec((None, None, 1, d), lambda i, j, l=l: (l, mod_row(i * INPROJ_TM), 0, 0)),
                pl.BlockSpec((None, 1, d), lambda i, j, l=l: (l, 0, 0)),
                pl.BlockSpec((None, d, INPROJ_TN), lambda i, j, l=l: (l, 0, jnp.where(i == 0, j, n_j - 1))),
            ],
            out_specs=pl.BlockSpec((INPROJ_TM, INPROJ_TN), lambda i, j: (i, j)),
            scratch_shapes=[pltpu.VMEM((INPROJ_TM, d), BF16), pltpu.VMEM((n_j, d, INPROJ_TN), BF16)],
            compiler_params=_cparams(("arbitrary", "arbitrary")),
            name="inproj",
        )(x, mods, mods, norm1_g.reshape(depth, 1, d), w_in)

        hb = CONV_TT // CONV_HALO
        n_hb = nt // CONV_HALO
        conv_out = pl.pallas_call(
            functools.partial(_conv_kernel, nctx=nctx, ctx_seq=ctx_seq, lat_seq=lat_seq, width=conv_width),
            out_shape=jax.ShapeDtypeStruct((nt, conv_ch), BF16),
            grid=(nt // CONV_TT,),
            in_specs=[
                pl.BlockSpec((CONV_TT, conv_ch), lambda i: (i, c_ca // conv_ch)),
                pl.BlockSpec((CONV_TT, conv_ch), lambda i: (i, c_cg // conv_ch)),
                pl.BlockSpec((CONV_HALO, conv_ch), lambda i: (jnp.maximum(i * hb - 1, 0), c_ca // conv_ch)),
                pl.BlockSpec((CONV_HALO, conv_ch), lambda i: (jnp.maximum(i * hb - 1, 0), c_cg // conv_ch)),
                pl.BlockSpec((CONV_HALO, conv_ch), lambda i: (jnp.minimum((i + 1) * hb, n_hb - 1), c_ca // conv_ch)),
                pl.BlockSpec((CONV_HALO, conv_ch), lambda i: (jnp.minimum((i + 1) * hb, n_hb - 1), c_cg // conv_ch)),
                pl.BlockSpec((None, conv_width, conv_ch), lambda i, l=l: (l, 0, 0)),
                pl.BlockSpec((None, 1, conv_ch), lambda i, l=l: (l, 0, 0)),
                pl.BlockSpec((None, 1, conv_ch), lambda i, l=l: (l, 0, 0)),
                pl.BlockSpec((None, 1, conv_ch), lambda i, l=l: (l, 0, 0)),
            ],
            out_specs=pl.BlockSpec((CONV_TT, conv_ch), lambda i: (i, 0)),
            scratch_shapes=[pltpu.VMEM((SUBLANES, CONV_TT + 2 * CONV_HALO, conv_ch), F32)],
            compiler_params=_cparams(("arbitrary",)),
            name="conv",
        )(proj, proj, proj, proj, proj, proj, conv_dw_w, conv_dw_b.reshape(depth, 1, conv_ch),
          conv_ln_g.reshape(depth, 1, conv_ch), conv_ln_b.reshape(depth, 1, conv_ch))

        def ret_call(nseq, seq, row0, has_s0, emit_state):
            nq = seq // RET_TQ
            in_specs = [
                smem,
                pl.BlockSpec((RET_TQ, ret_hd), lambda b, h, qi: (row0 // RET_TQ + b * nq + qi, c_rq // ret_hd + h)),
                pl.BlockSpec((seq, ret_hd), lambda b, h, qi: (row0 // seq + b, c_rk // ret_hd + h)),
                pl.BlockSpec((seq, ret_hd), lambda b, h, qi: (row0 // seq + b, c_rv // ret_hd + h)),
                pl.BlockSpec((RET_TQ, ret_hd), lambda b, h, qi: (row0 // RET_TQ + b * nq + qi, c_rg // ret_hd + h)),
            ]
            args = [lg, proj, proj, proj, proj]
            if has_s0:
                in_specs.append(pl.BlockSpec((None, None, 2, None, ret_hd, ret_hd), lambda b, h, qi, l=l: (b, l, 0, h, 0, 0)))
                args.append(state_retention)
            out_shape = [jax.ShapeDtypeStruct((nseq * seq, ret_w), BF16)]
            out_specs = [pl.BlockSpec((RET_TQ, ret_hd), lambda b, h, qi: (b * nq + qi, h))]
            if emit_state:
                out_shape.append(jax.ShapeDtypeStruct((nseq, 2, ret_heads, ret_hd, ret_hd), F32))
                out_specs.append(pl.BlockSpec((None, 2, None, ret_hd, ret_hd), lambda b, h, qi: (b, 0, h, 0, 0)))
            return pl.pallas_call(
                functools.partial(_ret_kernel, layer=l, heads=ret_heads, seq=seq, has_s0=has_s0, emit_state=emit_state),
                out_shape=out_shape,
                grid=(nseq, ret_heads, nq),
                in_specs=in_specs,
                out_specs=out_specs,
                compiler_params=_cparams(("arbitrary", "arbitrary", "arbitrary")),
                name="retention",
            )(*args)

        ret_ctx, s_l = ret_call(batch, ctx_seq, 0, False, True)
        (ret_lat,) = ret_call(dec_batch, lat_seq, nctx, True, False)
        s_list.append(s_l)

        gq = q_norm_g.reshape(depth, 1, att_hd)
        gk = k_norm_g.reshape(depth, 1, att_hd)
        qw = group * att_hd
        att_ctx, k_l, v_l = pl.pallas_call(
            functools.partial(_attn_ctx_kernel, layer=l, group=group, scale=att_hd ** -0.5),
            out_shape=[jax.ShapeDtypeStruct((nctx, att_w), BF16),
                       jax.ShapeDtypeStruct((batch, ctx_seq, kv_w), F32),
                       jax.ShapeDtypeStruct((batch, ctx_seq, kv_w), F32)],
            grid=(batch, kv_heads),
            in_specs=[
                smem,
                pl.BlockSpec((ctx_seq, qw), lambda b, kv: (b, c_aq // qw + kv)),
                pl.BlockSpec((ctx_seq, att_hd), lambda b, kv: (b, c_ak // att_hd + kv)),
                pl.BlockSpec((ctx_seq, att_hd), lambda b, kv: (b, c_av // att_hd + kv)),
                pl.BlockSpec((None, 1, att_hd), lambda b, kv, l=l: (l, 0, 0)),
                pl.BlockSpec((None, 1, att_hd), lambda b, kv, l=l: (l, 0, 0)),
            ],
            out_specs=[
                pl.BlockSpec((ctx_seq, qw), lambda b, kv: (b, kv)),
                pl.BlockSpec((None, ctx_seq, att_hd), lambda b, kv: (b, 0, kv)),
                pl.BlockSpec((None, ctx_seq, att_hd), lambda b, kv: (b, 0, kv)),
            ],
            compiler_params=_cparams(("arbitrary", "arbitrary")),
            name="attn_ctx",
        )(sinks, proj, proj, proj, gq, gk)
        k_list.append(k_l)
        v_list.append(v_l)

        nq = lat_seq // ATT_TQ
        att_lat = pl.pallas_call(
            functools.partial(_attn_lat_kernel, layer=l, group=group, scale=att_hd ** -0.5, seq=lat_seq, nkv=kv_heads),
            out_shape=jax.ShapeDtypeStruct((nlat, att_w), BF16),
            grid=(dec_batch, kv_heads, nq),
            in_specs=[
                smem,
                pl.BlockSpec((ATT_TQ, qw), lambda b, kv, qi: (nctx // ATT_TQ + b * nq + qi, c_aq // qw + kv)),
                pl.BlockSpec((lat_seq, att_hd), lambda b, kv, qi: (nctx // lat_seq + b, c_ak // att_hd + kv)),
                pl.BlockSpec((lat_seq, att_hd), lambda b, kv, qi: (nctx // lat_seq + b, c_av // att_hd + kv)),
                pl.BlockSpec((None, None, past_len, att_hd), lambda b, kv, qi, l=l: (b, l, 0, kv)),
                pl.BlockSpec((None, None, past_len, att_hd), lambda b, kv, qi, l=l: (b, l, 0, kv)),
                pl.BlockSpec((None, 1, att_hd), lambda b, kv, qi, l=l: (l, 0, 0)),
                pl.BlockSpec((None, 1, att_hd), lambda b, kv, qi, l=l: (l, 0, 0)),
                pl.BlockSpec((lat_seq, att_hd), lambda b, kv, qi: (0, 0)),
                pl.BlockSpec((lat_seq, att_hd), lambda b, kv, qi: (0, 0)),
            ],
            out_specs=pl.BlockSpec((ATT_TQ, qw), lambda b, kv, qi: (b * nq + qi, kv)),
            scratch_shapes=[pltpu.VMEM((lat_seq, att_hd), BF16)],
            compiler_params=_cparams(("arbitrary", "arbitrary", "arbitrary")),
            name="attn_lat",
        )(sinks, proj, proj, proj, kc_all, vc_all, gq, gk, cos_t, sin_t)

        ctx_tiles = nctx // OUT_TM
        lat_tiles = nlat // OUT_TM

        def ctx_rows(i):
            return jnp.minimum(i, ctx_tiles - 1)

        def lat_rows(i):
            return jnp.maximum(i - ctx_tiles, 0)

        x1, h2, route, cnt = pl.pallas_call(
            functools.partial(_outproj_kernel, n_groups=n_groups, n_experts=n_experts, ctx_tiles=ctx_tiles),
            out_shape=[jax.ShapeDtypeStruct((nt, d), F32), jax.ShapeDtypeStruct((nt * PACK_SUB, LANES), jnp.uint32),
                       jax.ShapeDtypeStruct((nt, LANES), F32), jax.ShapeDtypeStruct((1, LANES), F32)],
            grid=(ctx_tiles + lat_tiles,),
            in_specs=[
                pl.BlockSpec((OUT_TM, conv_ch), lambda i: (i, 0)),
                pl.BlockSpec((OUT_TM, ret_w), lambda i: (ctx_rows(i), 0)),
                pl.BlockSpec((OUT_TM, ret_w), lambda i: (lat_rows(i), 0)),
                pl.BlockSpec((OUT_TM, att_w), lambda i: (ctx_rows(i), 0)),
                pl.BlockSpec((OUT_TM, att_w), lambda i: (lat_rows(i), 0)),
                pl.BlockSpec((OUT_TM, d), lambda i: (i, 0)),
                mod_spec(l, 2, OUT_TM),
                mod_spec(l, 4, OUT_TM),
                mod_spec(l, 3, OUT_TM),
                pl.BlockSpec((None, 1, d), lambda i, l=l: (l, 0, 0)),
                pl.BlockSpec((None, d, d), lambda i, l=l: (l, 0, 0), pipeline_mode=pl.Buffered(1)),
                pl.BlockSpec((None, d, LANES), lambda i, l=l: (l, 0, 0), pipeline_mode=pl.Buffered(1)),
                pl.BlockSpec((None, 1, LANES), lambda i, l=l: (l, 0, 0)),
            ],
            out_specs=[
                pl.BlockSpec((OUT_TM, d), lambda i: (i, 0)),
                pl.BlockSpec((OUT_TM * PACK_SUB, LANES), lambda i: (i, 0)),
                pl.BlockSpec((OUT_TM, LANES), lambda i: (i, 0)),
                pl.BlockSpec((1, LANES), lambda i: (0, 0)),
            ],
            scratch_shapes=[pltpu.VMEM((d, d), BF16), pltpu.VMEM((d, LANES), BF16), pltpu.VMEM((d, LANES), BF16),
                            pltpu.VMEM((1, LANES), F32)],
            compiler_params=_cparams(("arbitrary",)),
            name="outproj_router",
        )(conv_out, ret_ctx, ret_lat, att_ctx, att_lat, x, mods, mods, mods, norm2_g.reshape(depth, 1, d), w_out, rw, rb)

        eid = route[:, 0:EXPERT_TOPK].astype(I32)
        rank = route[:, 4:4 + EXPERT_TOPK].astype(I32)
        counts = cnt[0, :n_experts].astype(I32)
        padded = ((counts + MOE_TM - 1) // MOE_TM) * MOE_TM
        pends = jnp.cumsum(padded)
        pstarts = pends - padded
        ek = jnp.arange(n_experts, dtype=I32)
        dest = rank + jnp.sum(jnp.where(eid[:, :, None] == ek, pstarts, 0), axis=-1)
        tok = jnp.broadcast_to(jnp.arange(nt, dtype=I32)[:, None], (nt, EXPERT_TOPK))
        row_tok = jnp.zeros((p_rows,), I32).at[dest.reshape(-1)].set(
            tok.reshape(-1), unique_indices=True, mode='promise_in_bounds')
        blk_start = jnp.arange(n_blk, dtype=I32) * MOE_TM
        blk_e = jnp.minimum(jnp.sum((pends[None, :] <= blk_start[:, None]).astype(I32), axis=1), n_experts - 1)
        later = (ek[None, :] > ek[:, None]) & (counts[None, :] > 0)
        nxt_e = jnp.min(jnp.where(later, ek[None, :], n_experts), axis=1)
        nxt_e = jnp.where(nxt_e == n_experts, -1, nxt_e).astype(I32)
        nused = (pends[-1:] // MOE_TM).astype(I32)

        ys = pl.pallas_call(
            functools.partial(_expert_kernel, layer=l),
            out_shape=jax.ShapeDtypeStruct((p_rows * PACK_SUB, LANES), jnp.uint32),
            grid_spec=pltpu.PrefetchScalarGridSpec(
                num_scalar_prefetch=4,
                grid=(n_blk,),
                in_specs=[hbm, hbm, hbm, hbm],
                out_specs=pl.BlockSpec((MOE_TM * PACK_SUB, LANES), lambda b, be, rt, nx, nu: (b, 0)),
                scratch_shapes=[pltpu.VMEM((2, MOE_TM * PACK_SUB, LANES), jnp.uint32), pltpu.VMEM((MOE_TM, d), BF16),
                                pltpu.VMEM((d, d_expert), F32), pltpu.VMEM((d, d_expert), F32),
                                pltpu.VMEM((d_expert, d), F32),
                                pltpu.VMEM((d, d_expert), BF16), pltpu.VMEM((d, d_expert), BF16),
                                pltpu.VMEM((d_expert, d), BF16),
                                pltpu.SemaphoreType.DMA((2,)), pltpu.SemaphoreType.DMA((3,))],
            ),
            compiler_params=_cparams(("arbitrary",)),
            name="experts",
        )(blk_e, row_tok, nxt_e, nused, h2, expert_w_gate, expert_w_up, expert_w_down)

        x = pl.pallas_call(
            _combine_kernel,
            out_shape=jax.ShapeDtypeStruct((nt, d), F32),
            grid_spec=pltpu.PrefetchScalarGridSpec(
                num_scalar_prefetch=2,
                grid=(nt // CMB_TT,),
                in_specs=[
                    hbm,
                    pl.BlockSpec((CMB_TT, d), lambda i, d0, d1: (i, 0)),
                    pl.BlockSpec((CMB_TT, LANES), lambda i, d0, d1: (i, 0)),
                    pl.BlockSpec((None, None, 1, d), lambda i, d0, d1, l=l: (l, mod_row(i * CMB_TT), 0, 5)),
                ],
                out_specs=pl.BlockSpec((CMB_TT, d), lambda i, d0, d1: (i, 0)),
                scratch_shapes=[pltpu.VMEM((2, EXPERT_TOPK, CMB_TT * PACK_SUB, LANES), jnp.uint32),
                                pltpu.SemaphoreType.DMA((2,))],
            ),
            compiler_params=_cparams(("arbitrary",)),
            name="combine",
        )(dest[:, 0], dest[:, 1], ys, x1, route, mods)

    y_prompt = x[:nctx].reshape(batch, ctx_seq, d)
    y_sample = x[nctx:].reshape(dec_batch, lat_seq, d)
    new_k = jnp.stack(k_list, axis=1).reshape(batch, depth, ctx_seq, kv_heads, att_hd)
    new_v = jnp.stack(v_list, axis=1).reshape(batch, depth, ctx_seq, kv_heads, att_hd)
    new_s = jnp.stack(s_list, axis=1)
    return (y_prompt, y_sample, new_k, new_v, new_s)
```

```python
import functools

import jax
import jax.numpy as jnp
from jax import lax
from jax.experimental import pallas as pl
from jax.experimental.pallas import tpu as pltpu

F32, BF16, I32 = jnp.float32, jnp.bfloat16, jnp.int32

GRID_W = 64
ATT_WINDOW = 128
ROPE_BASE = 10000.0
EXPERT_TOPK = 2
EPS = 1e-6
NEG_INF = -1e30

LANES = 128
SUBLANES = 8
VMEM_LIMIT_BYTES = 56 * 1024 * 1024

MOD_TN = 1024
INPROJ_TM = 1024
INPROJ_TN = 512
CONV_TT = 256
CONV_HALO = 16
CONV_RC = 32
RET_TQ = 256
ATT_TQ = 256
OUT_TM = 256
MOE_TM = 128
CMB_TT = 256
GATHER_UNROLL = 8
SCALAR_UNROLL = 16


def _cparams(sem):
    return pltpu.CompilerParams(dimension_semantics=sem, vmem_limit_bytes=VMEM_LIMIT_BYTES)


def _rms(x, g):
    return x * lax.rsqrt(jnp.mean(x * x, axis=-1, keepdims=True) + EPS) * g


def _silu(x):
    return x * jax.nn.sigmoid(x)


def _dot(a, b):
    return jnp.dot(a, b, preferred_element_type=F32)


def _dot_nt(a, b):
    return lax.dot_general(a, b, (((1,), (1,)), ((), ())), preferred_element_type=F32)


def _dot_tn(a, b):
    return lax.dot_general(a, b, (((0,), (0,)), ((), ())), preferred_element_type=F32)


def _mod_kernel(cv_ref, w_ref, b_ref, o_ref):
    s = _silu(cv_ref[...]).astype(BF16)
    o_ref[...] = _dot(s, w_ref[...].astype(BF16)) + b_ref[...]


def _modulation(cv, w_mod, b_mod):
    depth, d, n6 = w_mod.shape
    rows = cv.shape[0]
    return pl.pallas_call(
        _mod_kernel,
        out_shape=jax.ShapeDtypeStruct((depth, rows, n6), F32),
        grid=(depth, n6 // MOD_TN),
        in_specs=[
            pl.BlockSpec((rows, d), lambda l, j: (0, 0)),
            pl.BlockSpec((None, d, MOD_TN), lambda l, j: (l, 0, j)),
            pl.BlockSpec((None, 1, MOD_TN), lambda l, j: (l, 0, j)),
        ],
        out_specs=pl.BlockSpec((None, rows, MOD_TN), lambda l, j: (l, 0, j)),
        compiler_params=_cparams(("arbitrary", "arbitrary")),
        name="modulation",
    )(cv, w_mod, b_mod.reshape(depth, 1, n6))


def _inproj_kernel(x_ref, sc_ref, sh_ref, g_ref, w_ref, o_ref, h_ref, wb_ref):
    i = pl.program_id(0)
    j = pl.program_id(1)

    @pl.when(i == 0)
    def _():
        wb_ref[j] = w_ref[...].astype(BF16)

    @pl.when(j == 0)
    def _():
        h = _rms(x_ref[...], g_ref[...]) * (1.0 + sc_ref[...]) + sh_ref[...]
        h_ref[...] = h.astype(BF16)

    o_ref[...] = _dot(h_ref[...], wb_ref[j]).astype(BF16)


def _conv_kernel(a_ref, g_ref, ap_ref, gp_ref, an_ref, gn_ref, w_ref, b_ref, lng_ref, lnb_ref, o_ref,
                 hs_ref, *, nctx, ctx_seq, lat_seq, width):
    t0 = pl.program_id(0) * CONV_TT
    is_ctx = t0 < nctx
    seq_len = jnp.where(is_ctx, ctx_seq, lat_seq)
    pos = lax.rem(jnp.where(is_ctx, t0, t0 - nctx), seq_len)
    first = pos == 0
    last = pos + CONV_TT == seq_len
    rows = CONV_TT + 2 * CONV_HALO
    span = rows - SUBLANES
    def glu(a, g):
        return a[...].astype(F32) * jax.nn.sigmoid(g[...].astype(F32))

    hs_ref[0, 0:CONV_HALO, :] = jnp.where(first, 0.0, glu(ap_ref, gp_ref))
    hs_ref[0, CONV_HALO:CONV_HALO + CONV_TT, :] = glu(a_ref, g_ref)
    hs_ref[0, CONV_HALO + CONV_TT:rows, :] = jnp.where(last, 0.0, glu(an_ref, gn_ref))
    for s in range(1, SUBLANES):
        hs_ref[s, 0:span, :] = hs_ref[0, s:s + span, :]
    base = CONV_HALO - width // 2
    for r0 in range(0, CONV_TT, CONV_RC):
        acc = jnp.zeros((CONV_RC, o_ref.shape[1]), F32)
        for k in range(width):
            idx = k + base
            s, q = idx % SUBLANES, idx // SUBLANES
            acc = acc + w_ref[k:k + 1, :] * hs_ref[s, r0 + SUBLANES * q:r0 + SUBLANES * q + CONV_RC, :]
        hf = acc + b_ref[...]
        mu = jnp.mean(hf, axis=-1, keepdims=True)
        var = jnp.mean(jnp.square(hf - mu), axis=-1, keepdims=True)
        y = (hf - mu) * lax.rsqrt(var + EPS) * lng_ref[...] + lnb_ref[...]
        o_ref[r0:r0 + CONV_RC, :] = _silu(y).astype(BF16)


def _ret_kernel(lg_ref, q_ref, k_ref, v_ref, gate_ref, *rest, layer, heads, seq, has_s0, emit_state):
    rest = list(rest)
    s0_ref = rest.pop(0) if has_s0 else None
    o_ref = rest.pop(0)
    st_ref = rest.pop(0) if emit_state else None
    h = pl.program_id(1)
    qi = pl.program_id(2)
    lgf = lg_ref[layer * 2 * heads + h]
    lgb = lg_ref[layer * 2 * heads + heads + h]
    dh = q_ref.shape[1]
    q = q_ref[...].astype(F32)
    k = k_ref[...].astype(F32) * (dh ** -0.5)
    vb = v_ref[...]
    a = _dot_nt(q_ref[...], k.astype(BF16))
    n_idx = qi * RET_TQ + lax.broadcasted_iota(I32, a.shape, 0)
    m_idx = lax.broadcasted_iota(I32, a.shape, 1)
    rel = (n_idx - m_idx).astype(F32)
    decay = jnp.exp(jnp.where(rel >= 0.0, lgf, -lgb) * rel)
    o = _dot((a * decay).astype(BF16), vb)
    if has_s0:
        npos = (qi * RET_TQ + lax.broadcasted_iota(I32, (RET_TQ, 1), 0)).astype(F32)
        cf = jnp.exp(lgf * (npos + 1.0))
        cb = jnp.exp(lgb * (seq - npos))
        o = o + _dot((q * cf).astype(BF16), s0_ref[0].astype(BF16)) + _dot((q * cb).astype(BF16), s0_ref[1].astype(BF16))
    y = o * lax.rsqrt(jnp.mean(o * o, axis=-1, keepdims=True) + EPS)
    o_ref[...] = (_silu(gate_ref[...].astype(F32)) * y).astype(BF16)
    if emit_state:
        m = lax.broadcasted_iota(I32, (seq, 1), 0).astype(F32)
        st_ref[0] = _dot_tn((k * jnp.exp(lgf * (seq - 1.0 - m))).astype(BF16), vb)
        st_ref[1] = _dot_tn((k * jnp.exp(lgb * m)).astype(BF16), vb)


def _attn_ctx_kernel(sink_ref, q_ref, k_ref, v_ref, gq_ref, gk_ref, o_ref, ko_ref, vo_ref, *, layer, group, scale):
    kv = pl.program_id(1)
    dh = k_ref.shape[1]
    kn = _rms(k_ref[...].astype(F32), gk_ref[...])
    vb = v_ref[...]
    ko_ref[...] = kn
    vo_ref[...] = vb.astype(F32)
    knb = kn.astype(BF16)
    for g in range(group):
        sink = sink_ref[layer * pl.num_programs(1) * group + kv * group + g]
        qn = _rms(q_ref[:, g * dh:(g + 1) * dh].astype(F32), gq_ref[...])
        s = _dot_nt(qn.astype(BF16), knb) * scale
        m = jnp.maximum(jnp.max(s, axis=-1, keepdims=True), sink)
        p = jnp.exp(s - m)
        den = jnp.sum(p, axis=-1, keepdims=True) + jnp.exp(sink - m)
        o_ref[:, g * dh:(g + 1) * dh] = (_dot(p.astype(BF16), vb) / den).astype(BF16)


def _rope(x, cos, sin_signed):
    lane = lax.broadcasted_iota(I32, x.shape, 1)
    half = x.shape[1] // 4
    swapped = jnp.where(lax.rem(lane, 2 * half) < half, pltpu.roll(x, x.shape[1] - half, 1), pltpu.roll(x, half, 1))
    return x * cos + swapped * sin_signed


def _attn_lat_kernel(sink_ref, q_ref, k_ref, v_ref, kc_ref, vc_ref, gq_ref, gk_ref, cos_ref, sin_ref,
                     o_ref, kr_ref, *, layer, group, scale, seq, nkv):
    kv = pl.program_id(1)
    qi = pl.program_id(2)
    dh = k_ref.shape[1]
    win = ATT_TQ + 2 * ATT_WINDOW

    @pl.when(qi == 0)
    def _():
        kr_ref[...] = _rope(_rms(k_ref[...].astype(F32), gk_ref[...]), cos_ref[...], sin_ref[...]).astype(BF16)

    q0 = pl.multiple_of(qi * ATT_TQ, ATT_TQ)
    start = pl.multiple_of(jnp.clip(qi * ATT_TQ - ATT_WINDOW, 0, seq - win), ATT_WINDOW)
    kw = kr_ref[pl.ds(start, win), :]
    vw = v_ref[pl.ds(start, win), :]
    kcb = kc_ref[...].astype(BF16)
    vcb = vc_ref[...].astype(BF16)
    qpos = q0 + lax.broadcasted_iota(I32, (ATT_TQ, win), 0)
    kpos = start + lax.broadcasted_iota(I32, (ATT_TQ, win), 1)
    valid = jnp.abs(qpos - kpos) <= ATT_WINDOW
    cs = cos_ref[pl.ds(q0, ATT_TQ), :]
    sn = sin_ref[pl.ds(q0, ATT_TQ), :]
    for g in range(group):
        sink = sink_ref[layer * nkv * group + kv * group + g]
        qr = _rope(_rms(q_ref[:, g * dh:(g + 1) * dh].astype(F32), gq_ref[...]), cs, sn).astype(BF16)
        s_loc = jnp.where(valid, _dot_nt(qr, kw) * scale, NEG_INF)
        s_ctx = _dot_nt(qr, kcb) * scale
        m = jnp.maximum(jnp.maximum(jnp.max(s_loc, axis=-1, keepdims=True), jnp.max(s_ctx, axis=-1, keepdims=True)), sink)
        p_loc = jnp.exp(s_loc - m)
        p_ctx = jnp.exp(s_ctx - m)
        den = jnp.sum(p_loc, axis=-1, keepdims=True) + jnp.sum(p_ctx, axis=-1, keepdims=True) + jnp.exp(sink - m)
        o = _dot(p_loc.astype(BF16), vw) + _dot(p_ctx.astype(BF16), vcb)
        o_ref[:, g * dh:(g + 1) * dh] = (o / den).astype(BF16)


def _outproj_kernel(conv_ref, retc_ref, retl_ref, attc_ref, attl_ref, x_ref, g1_ref, sc2_ref, sh2_ref, n2_ref, wout_ref,
                    rw_ref, rb_ref, x1_ref, h2_ref, route_ref, cnt_ref, wb_ref, rwb_ref, run_ref,
                    *, n_groups, n_experts, ctx_tiles):
    s = pl.program_id(0)
    c_conv = conv_ref.shape[1]
    c_ret = retc_ref.shape[1]
    is_ctx = s < ctx_tiles
    ret = jnp.where(is_ctx, retc_ref[...], retl_ref[...])
    att = jnp.where(is_ctx, attc_ref[...], attl_ref[...])

    @pl.when(s == 0)
    def _():
        wb_ref[...] = wout_ref[...].astype(BF16)
        rwb_ref[...] = rw_ref[...].astype(BF16)
        run_ref[...] = jnp.zeros_like(run_ref)

    def step():
        mix = (_dot(conv_ref[...], wb_ref[0:c_conv, :]) + _dot(ret, wb_ref[c_conv:c_conv + c_ret, :])
               + _dot(att, wb_ref[c_conv + c_ret:, :]))
        x1 = x_ref[...] + g1_ref[...] * mix
        x1_ref[...] = x1
        h2 = _rms(x1, n2_ref[...]) * (1.0 + sc2_ref[...]) + sh2_ref[...]
        _pack_rows(h2, h2_ref)
        lg = _dot(h2.astype(BF16), rwb_ref[...]) + rb_ref[...]
        tm = lg.shape[0]
        lane = lax.broadcasted_iota(I32, lg.shape, 1)
        lanef = lane.astype(F32)
        big = float(LANES)
        gmask = (lane >= n_experts) & (lane < n_experts + n_groups)
        gl = jnp.where(gmask, lg, NEG_INF)
        gmax = jnp.max(gl, axis=-1, keepdims=True)
        gidx = jnp.min(jnp.where(gl == gmax, lanef, big), axis=-1, keepdims=True) - float(n_experts)
        gp = 1.0 / jnp.sum(jnp.where(gmask, jnp.exp(gl - gmax), 0.0), axis=-1, keepdims=True)
        per_group = n_experts // n_groups
        emask = (lanef >= gidx * per_group) & (lanef < (gidx + 1.0) * per_group)
        el = jnp.where(emask, lg, NEG_INF)
        v1 = jnp.max(el, axis=-1, keepdims=True)
        i1 = jnp.min(jnp.where(el == v1, lanef, big), axis=-1, keepdims=True)
        el2 = jnp.where(lanef == i1, NEG_INF, el)
        v2 = jnp.max(el2, axis=-1, keepdims=True)
        i2 = jnp.min(jnp.where(el2 == v2, lanef, big), axis=-1, keepdims=True)
        t = jnp.exp(v2 - v1)
        w1 = gp / (1.0 + t)
        w2 = gp * t / (1.0 + t)
        sel1 = lanef == i1
        sel2 = lanef == i2
        oh = jnp.where(sel1 | sel2, 1.0, 0.0)
        r_i = lax.broadcasted_iota(I32, (tm, tm), 0)
        c_i = lax.broadcasted_iota(I32, (tm, tm), 1)
        lower = jnp.where(c_i < r_i, 1.0, 0.0).astype(BF16)
        before = _dot(lower, oh.astype(BF16)) + run_ref[...]
        rank1 = jnp.sum(jnp.where(sel1, before, 0.0), axis=-1, keepdims=True)
        rank2 = jnp.sum(jnp.where(sel2, before, 0.0), axis=-1, keepdims=True)
        run_ref[...] = run_ref[...] + jnp.sum(oh, axis=0, keepdims=True)
        cnt_ref[...] = run_ref[...]
        route = jnp.where(lane == 0, i1, jnp.where(lane == 1, i2, jnp.where(lane == 2, w1, jnp.where(
            lane == 3, w2, jnp.where(lane == 4, rank1, jnp.where(lane == 5, rank2, 0.0))))))
        route_ref[...] = route

    step()


PACK_SUB = SUBLANES


def _pack_rows(v, o_ref):
    tm = v.shape[0]
    half = PACK_SUB * LANES
    bits = pltpu.bitcast(v.astype(BF16).astype(F32), jnp.uint32)
    for c in range(PACK_SUB):
        lo = lax.shift_right_logical(bits[:, c * LANES:(c + 1) * LANES], jnp.uint32(16))
        hi = bits[:, half + c * LANES:half + (c + 1) * LANES] & jnp.uint32(0xFFFF0000)
        o_ref[pl.ds(c, tm, stride=PACK_SUB), :] = lo | hi


def _unpack_chunk(src_ref, c, rows):
    w = src_ref[pl.ds(c, rows, stride=PACK_SUB), :]
    lo = pltpu.bitcast(lax.shift_left(w, jnp.uint32(16)), F32)
    hi = pltpu.bitcast(w & jnp.uint32(0xFFFF0000), F32)
    return lo, hi


def _expert_kernel(blk_e_ref, d0_ref, d1_ref, nxt_e_ref, nused_ref, h2_hbm, wg_hbm, wu_hbm, wd_hbm, ys_ref,
                   row_tok_ref, xbuf, xb, wg_stg, wu_stg, wd_stg, wgb, wub, wdb, sem, wsem, *, layer):
    b = pl.program_id(0)
    nused = nused_ref[0]
    slot = lax.rem(b, 2)

    @pl.when(b == 0)
    def _():
        def clear(r, carry):
            row_tok_ref[r] = 0
            return carry

        def fill(t, carry):
            row_tok_ref[d0_ref[t]] = t
            row_tok_ref[d1_ref[t]] = t
            return carry

        lax.fori_loop(0, row_tok_ref.shape[0], clear, 0, unroll=SCALAR_UNROLL)
        lax.fori_loop(0, d0_ref.shape[0], fill, 0, unroll=SCALAR_UNROLL)

    def gather_start(blk, sl):
        def body(r, carry):
            src = pl.multiple_of(row_tok_ref[blk * MOE_TM + r] * PACK_SUB, PACK_SUB)
            dst = pl.multiple_of(r * PACK_SUB, PACK_SUB)
            pltpu.make_async_copy(h2_hbm.at[pl.ds(src, PACK_SUB), :], xbuf.at[sl, pl.ds(dst, PACK_SUB), :],
                                  sem.at[sl]).start()
            return carry

        lax.fori_loop(0, MOE_TM, body, 0, unroll=GATHER_UNROLL)

    def gather_wait(sl):
        pltpu.make_async_copy(h2_hbm.at[pl.ds(0, MOE_TM * PACK_SUB), :], xbuf.at[sl], sem.at[sl]).wait()

    def weight_copies(e):
        return (pltpu.make_async_copy(wg_hbm.at[layer, e], wg_stg, wsem.at[0]),
                pltpu.make_async_copy(wu_hbm.at[layer, e], wu_stg, wsem.at[1]),
                pltpu.make_async_copy(wd_hbm.at[layer, e], wd_stg, wsem.at[2]))

    @pl.when((b == 0) & (nused > 0))
    def _():
        gather_start(0, 0)
        for cp in weight_copies(blk_e_ref[0]):
            cp.start()

    @pl.when(b + 1 < nused)
    def _():
        gather_start(b + 1, 1 - slot)

    @pl.when(b < nused)
    def _():
        e = blk_e_ref[b]
        prev = blk_e_ref[jnp.maximum(b - 1, 0)]

        @pl.when((b == 0) | (e != prev))
        def _():
            for cp in weight_copies(e):
                cp.wait()
            wgb[...] = wg_stg[...].astype(BF16)
            wub[...] = wu_stg[...].astype(BF16)
            wdb[...] = wd_stg[...].astype(BF16)
            nxt = nxt_e_ref[e]

            @pl.when(nxt >= 0)
            def _():
                for cp in weight_copies(nxt):
                    cp.start()

        gather_wait(slot)
        half = PACK_SUB * LANES
        for c in range(PACK_SUB):
            lo, hi = _unpack_chunk(xbuf.at[slot], c, MOE_TM)
            xb[:, c * LANES:(c + 1) * LANES] = lo.astype(BF16)
            xb[:, half + c * LANES:half + (c + 1) * LANES] = hi.astype(BF16)
        x = xb[...]
        hid = _silu(_dot(x, wgb[...])) * _dot(x, wub[...])
        _pack_rows(_dot(hid.astype(BF16), wdb[...]), ys_ref)

    @pl.when(b >= nused)
    def _():
        ys_ref[...] = jnp.zeros_like(ys_ref)


def _combine_kernel(d0_ref, d1_ref, ys_hbm, x1_ref, route_ref, g2_ref, o_ref, ybuf, sem):
    i = pl.program_id(0)
    n = pl.num_programs(0)
    slot = lax.rem(i, 2)

    def gather_start(tile, sl):
        def body(r, carry):
            t = tile * CMB_TT + r
            dst = pl.multiple_of(r * PACK_SUB, PACK_SUB)
            for j, d_ref in enumerate((d0_ref, d1_ref)):
                src = pl.multiple_of(d_ref[t] * PACK_SUB, PACK_SUB)
                pltpu.make_async_copy(ys_hbm.at[pl.ds(src, PACK_SUB), :], ybuf.at[sl, j, pl.ds(dst, PACK_SUB), :],
                                      sem.at[sl]).start()
            return carry

        lax.fori_loop(0, CMB_TT, body, 0, unroll=GATHER_UNROLL)

    def gather_wait(sl):
        for j in range(EXPERT_TOPK):
            pltpu.make_async_copy(ys_hbm.at[pl.ds(0, CMB_TT * PACK_SUB), :], ybuf.at[sl, j], sem.at[sl]).wait()

    @pl.when(i == 0)
    def _():
        gather_start(0, 0)

    @pl.when(i + 1 < n)
    def _():
        gather_start(i + 1, 1 - slot)

    gather_wait(slot)
    w0 = route_ref[:, 2:3]
    w1 = route_ref[:, 3:4]
    half = PACK_SUB * LANES
    for c in range(PACK_SUB):
        lo0, hi0 = _unpack_chunk(ybuf.at[slot, 0], c, CMB_TT)
        lo1, hi1 = _unpack_chunk(ybuf.at[slot, 1], c, CMB_TT)
        for off, y0, y1 in ((c * LANES, lo0, lo1), (half + c * LANES, hi0, hi1)):
            cols = slice(off, off + LANES)
            o_ref[:, cols] = x1_ref[:, cols] + g2_ref[:, cols] * (y0 * w0 + y1 * w1)


def _rope_tables(seq, dh):
    nf = dh // 4
    t = jnp.arange(seq)
    inv = ROPE_BASE ** (-jnp.arange(nf, dtype=F32) / nf)
    ang_r = (t // GRID_W).astype(F32)[:, None] * inv[None, :]
    ang_c = (t % GRID_W).astype(F32)[:, None] * inv[None, :]
    cos = jnp.concatenate([jnp.cos(ang_r)] * 2 + [jnp.cos(ang_c)] * 2, axis=-1)
    sin = jnp.concatenate([-jnp.sin(ang_r), jnp.sin(ang_r), -jnp.sin(ang_c), jnp.sin(ang_c)], axis=-1)
    return cos, sin


def kernel(x_prompt, x_sample, cache_attn_k, cache_attn_v, state_retention, c, c_ctx, w_mod, b_mod, norm1_g, norm2_g, w_in, conv_dw_w, conv_dw_b, conv_ln_g, conv_ln_b, ret_decay, q_norm_g, k_norm_g, attn_sink, w_out, router_group_w, router_group_b, router_expert_w, router_expert_b, expert_w_gate, expert_w_up, expert_w_down):
    batch, ctx_seq, d = x_prompt.shape
    dec_batch, lat_seq, _ = x_sample.shape
    depth = w_mod.shape[0]
    in_width = w_in.shape[2]
    conv_width, conv_ch = conv_dw_w.shape[1:]
    ret_heads = ret_decay.shape[2]
    ret_hd = state_retention.shape[-1]
    att_heads = attn_sink.shape[1]
    kv_heads, att_hd = cache_attn_k.shape[3:]
    past_len = cache_attn_k.shape[2]
    group = att_heads // kv_heads
    n_groups = router_group_w.shape[2]
    n_experts = router_expert_w.shape[2]
    d_expert = expert_w_gate.shape[3]
    nctx, nlat = batch * ctx_seq, dec_batch * lat_seq
    nt = nctx + nlat
    ret_w = ret_heads * ret_hd
    att_w = att_heads * att_hd
    kv_w = kv_heads * att_hd
    c_ca, c_cg = 0, conv_ch
    c_rq = 2 * conv_ch
    c_rk, c_rv, c_rg = c_rq + ret_w, c_rq + 2 * ret_w, c_rq + 3 * ret_w
    c_aq = c_rq + 4 * ret_w
    c_ak = c_aq + att_w
    c_av = c_ak + kv_w
    assert c_av + kv_w == in_width
    assert ctx_seq == CONV_TT == RET_TQ and lat_seq % ATT_TQ == 0 and INPROJ_TM == lat_seq
    assert conv_width // 2 <= CONV_HALO and n_experts + n_groups <= LANES
    assert d == 2 * PACK_SUB * LANES

    mod_rows = SUBLANES
    assert 1 + dec_batch <= mod_rows
    cv = jnp.concatenate([c_ctx[None, :], c, jnp.zeros((mod_rows - 1 - dec_batch, d), F32)], axis=0)
    mods = _modulation(cv, w_mod, b_mod).reshape(depth, mod_rows, 1, 6 * d)

    def mod_row(t0):
        return jnp.where(t0 < nctx, 0, (t0 - nctx) // lat_seq + 1)

    lg = jax.nn.log_sigmoid(ret_decay.astype(F32)).reshape(-1)
    sinks = attn_sink.astype(F32).reshape(-1)
    cos_t, sin_t = _rope_tables(lat_seq, att_hd)
    rw = jnp.concatenate([router_expert_w, router_group_w, jnp.zeros((depth, d, LANES - n_experts - n_groups), F32)], axis=2)
    rb = jnp.concatenate([router_expert_b, router_group_b, jnp.zeros((depth, LANES - n_experts - n_groups), F32)], axis=1)
    rb = rb.reshape(depth, 1, LANES)
    kc_all = cache_attn_k.reshape(dec_batch, depth, past_len, kv_w)
    vc_all = cache_attn_v.reshape(dec_batch, depth, past_len, kv_w)
    n_assign = nt * EXPERT_TOPK
    p_rows = n_assign + n_experts * MOE_TM
    n_blk = p_rows // MOE_TM
    smem = pl.BlockSpec(memory_space=pltpu.SMEM)
    hbm = pl.BlockSpec(memory_space=pl.ANY)

    x = jnp.concatenate([x_prompt.reshape(nctx, d), x_sample.reshape(nlat, d)], axis=0)
    k_list, v_list, s_list = [], [], []
    for l in range(depth):
        n_j = in_width // INPROJ_TN
        proj = pl.pallas_call(
            _inproj_kernel,
            out_shape=jax.ShapeDtypeStruct((nt, in_width), BF16),
            grid=(nt // INPROJ_TM, n_j),
            in_specs=[
                pl.BlockSpec((INPROJ_TM, d), lambda i, j: (i, 0)),
                pl.BlockSpec((None, None, 1, d), lambda i, j, l=l: (l, mod_row(i * INPROJ_TM), 0, 1)),
                pl.BlockSpec((None, None, 1, d), lambda i, j, l=l: (l, mod_row(i * INPROJ_TM), 0, 0)),
                pl.BlockSpec((None, 1, d), lambda i, j, l=l: (l, 0, 0)),
                pl.BlockSpec((None, d, INPROJ_TN), lambda i, j, l=l: (l, 0, jnp.where(i == 0, j, n_j - 1))),
            ],
            out_specs=pl.BlockSpec((INPROJ_TM, INPROJ_TN), lambda i, j: (i, j)),
            scratch_shapes=[pltpu.VMEM((INPROJ_TM, d), BF16), pltpu.VMEM((n_j, d, INPROJ_TN), BF16)],
            compiler_params=_cparams(("arbitrary", "arbitrary")),
            name="inproj",
        )(x, mods, mods, norm1_g.reshape(depth, 1, d), w_in)

        hb = CONV_TT // CONV_HALO
        n_hb = nt // CONV_HALO
        conv_out = pl.pallas_call(
            functools.partial(_conv_kernel, nctx=nctx, ctx_seq=ctx_seq, lat_seq=lat_seq, width=conv_width),
            out_shape=jax.ShapeDtypeStruct((nt, conv_ch), BF16),
            grid=(nt // CONV_TT,),
            in_specs=[
                pl.BlockSpec((CONV_TT, conv_ch), lambda i: (i, c_ca // conv_ch)),
                pl.BlockSpec((CONV_TT, conv_ch), lambda i: (i, c_cg // conv_ch)),
                pl.BlockSpec((CONV_HALO, conv_ch), lambda i: (jnp.maximum(i * hb - 1, 0), c_ca // conv_ch)),
                pl.BlockSpec((CONV_HALO, conv_ch), lambda i: (jnp.maximum(i * hb - 1, 0), c_cg // conv_ch)),
                pl.BlockSpec((CONV_HALO, conv_ch), lambda i: (jnp.minimum((i + 1) * hb, n_hb - 1), c_ca // conv_ch)),
                pl.BlockSpec((CONV_HALO, conv_ch), lambda i: (jnp.minimum((i + 1) * hb, n_hb - 1), c_cg // conv_ch)),
                pl.BlockSpec((None, conv_width, conv_ch), lambda i, l=l: (l, 0, 0)),
                pl.BlockSpec((None, 1, conv_ch), lambda i, l=l: (l, 0, 0)),
                pl.BlockSpec((None, 1, conv_ch), lambda i, l=l: (l, 0, 0)),
                pl.BlockSpec((None, 1, conv_ch), lambda i, l=l: (l, 0, 0)),
            ],
            out_specs=pl.BlockSpec((CONV_TT, conv_ch), lambda i: (i, 0)),
            scratch_shapes=[pltpu.VMEM((SUBLANES, CONV_TT + 2 * CONV_HALO, conv_ch), F32)],
            compiler_params=_cparams(("arbitrary",)),
            name="conv",
        )(proj, proj, proj, proj, proj, proj, conv_dw_w, conv_dw_b.reshape(depth, 1, conv_ch),
          conv_ln_g.reshape(depth, 1, conv_ch), conv_ln_b.reshape(depth, 1, conv_ch))

        def ret_call(nseq, seq, row0, has_s0, emit_state):
            nq = seq // RET_TQ
            in_specs = [
                smem,
                pl.BlockSpec((RET_TQ, ret_hd), lambda b, h, qi: (row0 // RET_TQ + b * nq + qi, c_rq // ret_hd + h)),
                pl.BlockSpec((seq, ret_hd), lambda b, h, qi: (row0 // seq + b, c_rk // ret_hd + h)),
                pl.BlockSpec((seq, ret_hd), lambda b, h, qi: (row0 // seq + b, c_rv // ret_hd + h)),
                pl.BlockSpec((RET_TQ, ret_hd), lambda b, h, qi: (row0 // RET_TQ + b * nq + qi, c_rg // ret_hd + h)),
            ]
            args = [lg, proj, proj, proj, proj]
            if has_s0:
                in_specs.append(pl.BlockSpec((None, None, 2, None, ret_hd, ret_hd), lambda b, h, qi, l=l: (b, l, 0, h, 0, 0)))
                args.append(state_retention)
            out_shape = [jax.ShapeDtypeStruct((nseq * seq, ret_w), BF16)]
            out_specs = [pl.BlockSpec((RET_TQ, ret_hd), lambda b, h, qi: (b * nq + qi, h))]
            if emit_state:
                out_shape.append(jax.ShapeDtypeStruct((nseq, 2, ret_heads, ret_hd, ret_hd), F32))
                out_specs.append(pl.BlockSpec((None, 2, None, ret_hd, ret_hd), lambda b, h, qi: (b, 0, h, 0, 0)))
            return pl.pallas_call(
                functools.partial(_ret_kernel, layer=l, heads=ret_heads, seq=seq, has_s0=has_s0, emit_state=emit_state),
                out_shape=out_shape,
                grid=(nseq, ret_heads, nq),
                in_specs=in_specs,
                out_specs=out_specs,
                compiler_params=_cparams(("arbitrary", "arbitrary", "arbitrary")),
                name="retention",
            )(*args)

        ret_ctx, s_l = ret_call(batch, ctx_seq, 0, False, True)
        (ret_lat,) = ret_call(dec_batch, lat_seq, nctx, True, False)
        s_list.append(s_l)

        gq = q_norm_g.reshape(depth, 1, att_hd)
        gk = k_norm_g.reshape(depth, 1, att_hd)
        qw = group * att_hd
        att_ctx, k_l, v_l = pl.pallas_call(
            functools.partial(_attn_ctx_kernel, layer=l, group=group, scale=att_hd ** -0.5),
            out_shape=[jax.ShapeDtypeStruct((nctx, att_w), BF16),
                       jax.ShapeDtypeStruct((batch, ctx_seq, kv_w), F32),
                       jax.ShapeDtypeStruct((batch, ctx_seq, kv_w), F32)],
            grid=(batch, kv_heads),
            in_specs=[
                smem,
                pl.BlockSpec((ctx_seq, qw), lambda b, kv: (b, c_aq // qw + kv)),
                pl.BlockSpec((ctx_seq, att_hd), lambda b, kv: (b, c_ak // att_hd + kv)),
                pl.BlockSpec((ctx_seq, att_hd), lambda b, kv: (b, c_av // att_hd + kv)),
                pl.BlockSpec((None, 1, att_hd), lambda b, kv, l=l: (l, 0, 0)),
                pl.BlockSpec((None, 1, att_hd), lambda b, kv, l=l: (l, 0, 0)),
            ],
            out_specs=[
                pl.BlockSpec((ctx_seq, qw), lambda b, kv: (b, kv)),
                pl.BlockSpec((None, ctx_seq, att_hd), lambda b, kv: (b, 0, kv)),
                pl.BlockSpec((None, ctx_seq, att_hd), lambda b, kv: (b, 0, kv)),
            ],
            compiler_params=_cparams(("arbitrary", "arbitrary")),
            name="attn_ctx",
        )(sinks, proj, proj, proj, gq, gk)
        k_list.append(k_l)
        v_list.append(v_l)

        nq = lat_seq // ATT_TQ
        att_lat = pl.pallas_call(
            functools.partial(_attn_lat_kernel, layer=l, group=group, scale=att_hd ** -0.5, seq=lat_seq, nkv=kv_heads),
            out_shape=jax.ShapeDtypeStruct((nlat, att_w), BF16),
            grid=(dec_batch, kv_heads, nq),
            in_specs=[
                smem,
                pl.BlockSpec((ATT_TQ, qw), lambda b, kv, qi: (nctx // ATT_TQ + b * nq + qi, c_aq // qw + kv)),
                pl.BlockSpec((lat_seq, att_hd), lambda b, kv, qi: (nctx // lat_seq + b, c_ak // att_hd + kv)),
                pl.BlockSpec((lat_seq, att_hd), lambda b, kv, qi: (nctx // lat_seq + b, c_av // att_hd + kv)),
                pl.BlockSpec((None, None, past_len, att_hd), lambda b, kv, qi, l=l: (b, l, 0, kv)),
                pl.BlockSpec((None, None, past_len, att_hd), lambda b, kv, qi, l=l: (b, l, 0, kv)),
                pl.BlockSpec((None, 1, att_hd), lambda b, kv, qi, l=l: (l, 0, 0)),
                pl.BlockSpec((None, 1, att_hd), lambda b, kv, qi, l=l: (l, 0, 0)),
                pl.BlockSpec((lat_seq, att_hd), lambda b, kv, qi: (0, 0)),
                pl.BlockSpec((lat_seq, att_hd), lambda b, kv, qi: (0, 0)),
            ],
            out_specs=pl.BlockSpec((ATT_TQ, qw), lambda b, kv, qi: (b * nq + qi, kv)),
            scratch_shapes=[pltpu.VMEM((lat_seq, att_hd), BF16)],
            compiler_params=_cparams(("arbitrary", "arbitrary", "arbitrary")),
            name="attn_lat",
        )(sinks, proj, proj, proj, kc_all, vc_all, gq, gk, cos_t, sin_t)

        ctx_tiles = nctx // OUT_TM
        lat_tiles = nlat // OUT_TM

        def ctx_rows(i):
            return jnp.minimum(i, ctx_tiles - 1)

        def lat_rows(i):
            return jnp.maximum(i - ctx_tiles, 0)

        def tile_mod(chunk):
            return pl.BlockSpec((None, None, 1, d), lambda s, l=l: (l, mod_row(s * OUT_TM), 0, chunk))

        x1, h2, route, cnt = pl.pallas_call(
            functools.partial(_outproj_kernel, n_groups=n_groups, n_experts=n_experts, ctx_tiles=ctx_tiles),
            out_shape=[jax.ShapeDtypeStruct((nt, d), F32), jax.ShapeDtypeStruct((nt * PACK_SUB, LANES), jnp.uint32),
                       jax.ShapeDtypeStruct((nt, LANES), F32), jax.ShapeDtypeStruct((1, LANES), F32)],
            grid=(ctx_tiles + lat_tiles,),
            in_specs=[
                pl.BlockSpec((OUT_TM, conv_ch), lambda s: (s, 0)),
                pl.BlockSpec((OUT_TM, ret_w), lambda s: (ctx_rows(s), 0)),
                pl.BlockSpec((OUT_TM, ret_w), lambda s: (lat_rows(s), 0)),
                pl.BlockSpec((OUT_TM, att_w), lambda s: (ctx_rows(s), 0)),
                pl.BlockSpec((OUT_TM, att_w), lambda s: (lat_rows(s), 0)),
                pl.BlockSpec((OUT_TM, d), lambda s: (s, 0)),
                tile_mod(2),
                tile_mod(4),
                tile_mod(3),
                pl.BlockSpec((None, 1, d), lambda s, l=l: (l, 0, 0)),
                pl.BlockSpec((None, d, d), lambda s, l=l: (l, 0, 0), pipeline_mode=pl.Buffered(1)),
                pl.BlockSpec((None, d, LANES), lambda s, l=l: (l, 0, 0), pipeline_mode=pl.Buffered(1)),
                pl.BlockSpec((None, 1, LANES), lambda s, l=l: (l, 0, 0)),
            ],
            out_specs=[
                pl.BlockSpec((OUT_TM, d), lambda s: (s, 0)),
                pl.BlockSpec((OUT_TM * PACK_SUB, LANES), lambda s: (s, 0)),
                pl.BlockSpec((OUT_TM, LANES), lambda s: (s, 0)),
                pl.BlockSpec((1, LANES), lambda s: (0, 0)),
            ],
            scratch_shapes=[pltpu.VMEM((d, d), BF16), pltpu.VMEM((d, LANES), BF16), pltpu.VMEM((1, LANES), F32)],
            compiler_params=_cparams(("arbitrary",)),
            name="outproj_router",
        )(conv_out, ret_ctx, ret_lat, att_ctx, att_lat, x, mods, mods, mods, norm2_g.reshape(depth, 1, d), w_out, rw, rb)

        eid = route[:, 0:EXPERT_TOPK].astype(I32)
        rank = route[:, 4:4 + EXPERT_TOPK].astype(I32)
        counts = cnt[0, :n_experts].astype(I32)
        padded = ((counts + MOE_TM - 1) // MOE_TM) * MOE_TM
        pends = jnp.cumsum(padded)
        pstarts = pends - padded
        ek = jnp.arange(n_experts, dtype=I32)
        dest = rank + jnp.sum(jnp.where(eid[:, :, None] == ek, pstarts, 0), axis=-1)
        dest0, dest1 = dest[:, 0], dest[:, 1]
        blk_start = jnp.arange(n_blk, dtype=I32) * MOE_TM
        blk_e = jnp.minimum(jnp.sum((pends[None, :] <= blk_start[:, None]).astype(I32), axis=1), n_experts - 1)
        later = (ek[None, :] > ek[:, None]) & (counts[None, :] > 0)
        nxt_e = jnp.min(jnp.where(later, ek[None, :], n_experts), axis=1)
        nxt_e = jnp.where(nxt_e == n_experts, -1, nxt_e).astype(I32)
        nused = (pends[-1:] // MOE_TM).astype(I32)

        ys = pl.pallas_call(
            functools.partial(_expert_kernel, layer=l),
            out_shape=jax.ShapeDtypeStruct((p_rows * PACK_SUB, LANES), jnp.uint32),
            grid_spec=pltpu.PrefetchScalarGridSpec(
                num_scalar_prefetch=5,
                grid=(n_blk,),
                in_specs=[hbm, hbm, hbm, hbm],
                out_specs=pl.BlockSpec((MOE_TM * PACK_SUB, LANES), lambda b, *_: (b, 0)),
                scratch_shapes=[pltpu.SMEM((p_rows,), I32),
                                pltpu.VMEM((2, MOE_TM * PACK_SUB, LANES), jnp.uint32), pltpu.VMEM((MOE_TM, d), BF16),
                                pltpu.VMEM((d, d_expert), F32), pltpu.VMEM((d, d_expert), F32),
                                pltpu.VMEM((d_expert, d), F32),
                                pltpu.VMEM((d, d_expert), BF16), pltpu.VMEM((d, d_expert), BF16),
                                pltpu.VMEM((d_expert, d), BF16),
                                pltpu.SemaphoreType.DMA((2,)), pltpu.SemaphoreType.DMA((3,))],
            ),
            compiler_params=_cparams(("arbitrary",)),
            name="experts",
        )(blk_e, dest0, dest1, nxt_e, nused, h2, expert_w_gate, expert_w_up, expert_w_down)

        x = pl.pallas_call(
            _combine_kernel,
            out_shape=jax.ShapeDtypeStruct((nt, d), F32),
            grid_spec=pltpu.PrefetchScalarGridSpec(
                num_scalar_prefetch=2,
                grid=(nt // CMB_TT,),
                in_specs=[
                    hbm,
                    pl.BlockSpec((CMB_TT, d), lambda i, d0, d1: (i, 0)),
                    pl.BlockSpec((CMB_TT, LANES), lambda i, d0, d1: (i, 0)),
                    pl.BlockSpec((None, None, 1, d), lambda i, d0, d1, l=l: (l, mod_row(i * CMB_TT), 0, 5)),
                ],
                out_specs=pl.BlockSpec((CMB_TT, d), lambda i, d0, d1: (i, 0)),
                scratch_shapes=[pltpu.VMEM((2, EXPERT_TOPK, CMB_TT * PACK_SUB, LANES), jnp.uint32),
                                pltpu.SemaphoreType.DMA((2,))],
            ),
            compiler_params=_cparams(("arbitrary",)),
            name="combine",
        )(dest0, dest1, ys, x1, route, mods)

    y_prompt = x[:nctx].reshape(batch, ctx_seq, d)
    y_sample = x[nctx:].reshape(dec_batch, lat_seq, d)
    new_k = jnp.stack(k_list, axis=1).reshape(batch, depth, ctx_seq, kv_heads, att_hd)
    new_v = jnp.stack(v_list, axis=1).reshape(batch, depth, ctx_seq, kv_heads, att_hd)
    new_s = jnp.stack(s_list, axis=1)
    return (y_prompt, y_sample, new_k, new_v, new_s)
```

```python
import functools

import jax
import jax.numpy as jnp
from jax import lax
from jax.experimental import pallas as pl
from jax.experimental.pallas import tpu as pltpu

F32, BF16, I32 = jnp.float32, jnp.bfloat16, jnp.int32

GRID_W = 64
ATT_WINDOW = 128
ROPE_BASE = 10000.0
EXPERT_TOPK = 2
EPS = 1e-6
NEG_INF = -1e30

LANES = 128
SUBLANES = 8
VMEM_LIMIT_BYTES = 56 * 1024 * 1024

MOD_TN = 1024
INPROJ_TM = 1024
INPROJ_TN = 512
CONV_TT = 256
CONV_HALO = 16
CONV_RC = 32
RET_TQ = 256
ATT_TQ = 256
OUT_TM = 256
MOE_TM = 128
CMB_TT = 256
GATHER_UNROLL = 8
SCALAR_UNROLL = 16
WEIGHT_DMA_PRIORITY = 1


def _cparams(sem):
    return pltpu.CompilerParams(dimension_semantics=sem, vmem_limit_bytes=VMEM_LIMIT_BYTES)


def _rms(x, g):
    return x * lax.rsqrt(jnp.mean(x * x, axis=-1, keepdims=True) + EPS) * g


def _silu(x):
    return x * jax.nn.sigmoid(x)


def _dot(a, b):
    return jnp.dot(a, b, preferred_element_type=F32)


def _dot_nt(a, b):
    return lax.dot_general(a, b, (((1,), (1,)), ((), ())), preferred_element_type=F32)


def _dot_tn(a, b):
    return lax.dot_general(a, b, (((0,), (0,)), ((), ())), preferred_element_type=F32)


def _mod_kernel(cv_ref, w_ref, b_ref, o_ref):
    s = _silu(cv_ref[...]).astype(BF16)
    o_ref[...] = _dot(s, w_ref[...].astype(BF16)) + b_ref[...]


def _modulation(cv, w_mod, b_mod):
    depth, d, n6 = w_mod.shape
    rows = cv.shape[0]
    return pl.pallas_call(
        _mod_kernel,
        out_shape=jax.ShapeDtypeStruct((depth, rows, n6), F32),
        grid=(depth, n6 // MOD_TN),
        in_specs=[
            pl.BlockSpec((rows, d), lambda l, j: (0, 0)),
            pl.BlockSpec((None, d, MOD_TN), lambda l, j: (l, 0, j)),
            pl.BlockSpec((None, 1, MOD_TN), lambda l, j: (l, 0, j)),
        ],
        out_specs=pl.BlockSpec((None, rows, MOD_TN), lambda l, j: (l, 0, j)),
        compiler_params=_cparams(("arbitrary", "arbitrary")),
        name="modulation",
    )(cv, w_mod, b_mod.reshape(depth, 1, n6))


def _inproj_kernel(x_ref, sc_ref, sh_ref, g_ref, w_ref, o_ref, h_ref, wb_ref):
    i = pl.program_id(0)
    j = pl.program_id(1)

    @pl.when(i == 0)
    def _():
        wb_ref[j] = w_ref[...].astype(BF16)

    @pl.when(j == 0)
    def _():
        h = _rms(x_ref[...], g_ref[...]) * (1.0 + sc_ref[...]) + sh_ref[...]
        h_ref[...] = h.astype(BF16)

    o_ref[...] = _dot(h_ref[...], wb_ref[j]).astype(BF16)


def _conv_kernel(a_ref, g_ref, ap_ref, gp_ref, an_ref, gn_ref, w_ref, b_ref, lng_ref, lnb_ref, o_ref,
                 hs_ref, *, nctx, ctx_seq, lat_seq, width):
    t0 = pl.program_id(0) * CONV_TT
    is_ctx = t0 < nctx
    seq_len = jnp.where(is_ctx, ctx_seq, lat_seq)
    pos = lax.rem(jnp.where(is_ctx, t0, t0 - nctx), seq_len)
    first = pos == 0
    last = pos + CONV_TT == seq_len
    rows = CONV_TT + 2 * CONV_HALO
    span = rows - SUBLANES
    def glu(a, g):
        return a[...].astype(F32) * jax.nn.sigmoid(g[...].astype(F32))

    hs_ref[0, 0:CONV_HALO, :] = jnp.where(first, 0.0, glu(ap_ref, gp_ref))
    hs_ref[0, CONV_HALO:CONV_HALO + CONV_TT, :] = glu(a_ref, g_ref)
    hs_ref[0, CONV_HALO + CONV_TT:rows, :] = jnp.where(last, 0.0, glu(an_ref, gn_ref))
    for s in range(1, SUBLANES):
        hs_ref[s, 0:span, :] = hs_ref[0, s:s + span, :]
    base = CONV_HALO - width // 2
    for r0 in range(0, CONV_TT, CONV_RC):
        acc = jnp.zeros((CONV_RC, o_ref.shape[1]), F32)
        for k in range(width):
            idx = k + base
            s, q = idx % SUBLANES, idx // SUBLANES
            acc = acc + w_ref[k:k + 1, :] * hs_ref[s, r0 + SUBLANES * q:r0 + SUBLANES * q + CONV_RC, :]
        hf = acc + b_ref[...]
        mu = jnp.mean(hf, axis=-1, keepdims=True)
        var = jnp.mean(jnp.square(hf - mu), axis=-1, keepdims=True)
        y = (hf - mu) * lax.rsqrt(var + EPS) * lng_ref[...] + lnb_ref[...]
        o_ref[r0:r0 + CONV_RC, :] = _silu(y).astype(BF16)


def _ret_kernel(lg_ref, q_ref, k_ref, v_ref, gate_ref, *rest, layer, heads, seq, has_s0, emit_state):
    rest = list(rest)
    s0_ref = rest.pop(0) if has_s0 else None
    o_ref = rest.pop(0)
    st_ref = rest.pop(0) if emit_state else None
    qi = pl.program_id(1)
    dh = q_ref.shape[1] // heads
    n_idx = qi * RET_TQ + lax.broadcasted_iota(I32, (RET_TQ, seq), 0)
    m_idx = lax.broadcasted_iota(I32, (RET_TQ, seq), 1)
    rel = (n_idx - m_idx).astype(F32)
    npos = (qi * RET_TQ + lax.broadcasted_iota(I32, (RET_TQ, 1), 0)).astype(F32)
    mpos = lax.broadcasted_iota(I32, (seq, 1), 0).astype(F32)
    for h in range(heads):
        cols = slice(h * dh, (h + 1) * dh)
        lgf = lg_ref[layer * 2 * heads + h]
        lgb = lg_ref[layer * 2 * heads + heads + h]
        q = q_ref[:, cols].astype(F32)
        k = k_ref[:, cols].astype(F32) * (dh ** -0.5)
        vb = v_ref[:, cols]
        a = _dot_nt(q_ref[:, cols], k.astype(BF16))
        decay = jnp.exp(jnp.where(rel >= 0.0, lgf, -lgb) * rel)
        o = _dot((a * decay).astype(BF16), vb)
        if has_s0:
            cf = jnp.exp(lgf * (npos + 1.0))
            cb = jnp.exp(lgb * (seq - npos))
            o = (o + _dot((q * cf).astype(BF16), s0_ref[0, h].astype(BF16))
                 + _dot((q * cb).astype(BF16), s0_ref[1, h].astype(BF16)))
        y = o * lax.rsqrt(jnp.mean(o * o, axis=-1, keepdims=True) + EPS)
        o_ref[:, cols] = (_silu(gate_ref[:, cols].astype(F32)) * y).astype(BF16)
        if emit_state:
            st_ref[0, h] = _dot_tn((k * jnp.exp(lgf * (seq - 1.0 - mpos))).astype(BF16), vb)
            st_ref[1, h] = _dot_tn((k * jnp.exp(lgb * mpos)).astype(BF16), vb)


def _attn_ctx_kernel(sink_ref, q_ref, k_ref, v_ref, gq_ref, gk_ref, o_ref, ko_ref, vo_ref, *, layer, group, scale, nkv):
    dh = k_ref.shape[1] // nkv
    for kv in range(nkv):
        kcols = slice(kv * dh, (kv + 1) * dh)
        kn = _rms(k_ref[:, kcols].astype(F32), gk_ref[...])
        vb = v_ref[:, kcols]
        ko_ref[:, kcols] = kn
        vo_ref[:, kcols] = vb.astype(F32)
        knb = kn.astype(BF16)
        for g in range(group):
            head = kv * group + g
            cols = slice(head * dh, (head + 1) * dh)
            sink = sink_ref[layer * nkv * group + head]
            qn = _rms(q_ref[:, cols].astype(F32), gq_ref[...])
            s = _dot_nt(qn.astype(BF16), knb) * scale
            m = jnp.maximum(jnp.max(s, axis=-1, keepdims=True), sink)
            p = jnp.exp(s - m)
            den = jnp.sum(p, axis=-1, keepdims=True) + jnp.exp(sink - m)
            o_ref[:, cols] = (_dot(p.astype(BF16), vb) / den).astype(BF16)


def _rope(x, cos, sin_signed):
    lane = lax.broadcasted_iota(I32, x.shape, 1)
    half = x.shape[1] // 4
    swapped = jnp.where(lax.rem(lane, 2 * half) < half, pltpu.roll(x, x.shape[1] - half, 1), pltpu.roll(x, half, 1))
    return x * cos + swapped * sin_signed


def _attn_lat_kernel(sink_ref, q_ref, k_ref, v_ref, kc_ref, vc_ref, gq_ref, gk_ref, cos_ref, sin_ref,
                     o_ref, kr_ref, *, layer, group, scale, seq, nkv):
    qi = pl.program_id(1)
    dh = k_ref.shape[1] // nkv
    win = ATT_TQ + 2 * ATT_WINDOW

    @pl.when(qi == 0)
    def _():
        for kv in range(nkv):
            kcols = slice(kv * dh, (kv + 1) * dh)
            kn = _rms(k_ref[:, kcols].astype(F32), gk_ref[...])
            kr_ref[:, kcols] = _rope(kn, cos_ref[...], sin_ref[...]).astype(BF16)

    q0 = pl.multiple_of(qi * ATT_TQ, ATT_TQ)
    start = pl.multiple_of(jnp.clip(qi * ATT_TQ - ATT_WINDOW, 0, seq - win), ATT_WINDOW)
    qpos = q0 + lax.broadcasted_iota(I32, (ATT_TQ, win), 0)
    kpos = start + lax.broadcasted_iota(I32, (ATT_TQ, win), 1)
    valid = jnp.abs(qpos - kpos) <= ATT_WINDOW
    cs = cos_ref[pl.ds(q0, ATT_TQ), :]
    sn = sin_ref[pl.ds(q0, ATT_TQ), :]
    for kv in range(nkv):
        kcols = slice(kv * dh, (kv + 1) * dh)
        kw = kr_ref[pl.ds(start, win), kcols]
        vw = v_ref[pl.ds(start, win), kcols]
        kcb = kc_ref[:, kcols].astype(BF16)
        vcb = vc_ref[:, kcols].astype(BF16)
        for g in range(group):
            head = kv * group + g
            cols = slice(head * dh, (head + 1) * dh)
            sink = sink_ref[layer * nkv * group + head]
            qr = _rope(_rms(q_ref[:, cols].astype(F32), gq_ref[...]), cs, sn).astype(BF16)
            s_loc = jnp.where(valid, _dot_nt(qr, kw) * scale, NEG_INF)
            s_ctx = _dot_nt(qr, kcb) * scale
            m = jnp.maximum(jnp.maximum(jnp.max(s_loc, axis=-1, keepdims=True),
                                        jnp.max(s_ctx, axis=-1, keepdims=True)), sink)
            p_loc = jnp.exp(s_loc - m)
            p_ctx = jnp.exp(s_ctx - m)
            den = jnp.sum(p_loc, axis=-1, keepdims=True) + jnp.sum(p_ctx, axis=-1, keepdims=True) + jnp.exp(sink - m)
            o = _dot(p_loc.astype(BF16), vw) + _dot(p_ctx.astype(BF16), vcb)
            o_ref[:, cols] = (o / den).astype(BF16)


def _outproj_kernel(conv_ref, retc_ref, retl_ref, attc_ref, attl_ref, x_ref, g1_ref, sc2_ref, sh2_ref, n2_ref, wout_ref,
                    rw_ref, rb_ref, x1_ref, h2_ref, route_ref, cnt_ref, wb_ref, rwb_ref, run_ref,
                    *, n_groups, n_experts, ctx_tiles):
    s = pl.program_id(0)
    c_conv = conv_ref.shape[1]
    c_ret = retc_ref.shape[1]
    is_ctx = s < ctx_tiles
    ret = jnp.where(is_ctx, retc_ref[...], retl_ref[...])
    att = jnp.where(is_ctx, attc_ref[...], attl_ref[...])

    @pl.when(s == 0)
    def _():
        wb_ref[...] = wout_ref[...].astype(BF16)
        rwb_ref[...] = rw_ref[...].astype(BF16)
        run_ref[...] = jnp.zeros_like(run_ref)

    def step():
        mix = (_dot(conv_ref[...], wb_ref[0:c_conv, :]) + _dot(ret, wb_ref[c_conv:c_conv + c_ret, :])
               + _dot(att, wb_ref[c_conv + c_ret:, :]))
        x1 = x_ref[...] + g1_ref[...] * mix
        x1_ref[...] = x1
        h2 = _rms(x1, n2_ref[...]) * (1.0 + sc2_ref[...]) + sh2_ref[...]
        _pack_rows(h2, h2_ref)
        lg = _dot(h2.astype(BF16), rwb_ref[...]) + rb_ref[...]
        tm = lg.shape[0]
        lane = lax.broadcasted_iota(I32, lg.shape, 1)
        lanef = lane.astype(F32)
        big = float(LANES)
        gmask = (lane >= n_experts) & (lane < n_experts + n_groups)
        gl = jnp.where(gmask, lg, NEG_INF)
        gmax = jnp.max(gl, axis=-1, keepdims=True)
        gidx = jnp.min(jnp.where(gl == gmax, lanef, big), axis=-1, keepdims=True) - float(n_experts)
        gp = 1.0 / jnp.sum(jnp.where(gmask, jnp.exp(gl - gmax), 0.0), axis=-1, keepdims=True)
        per_group = n_experts // n_groups
        emask = (lanef >= gidx * per_group) & (lanef < (gidx + 1.0) * per_group)
        el = jnp.where(emask, lg, NEG_INF)
        v1 = jnp.max(el, axis=-1, keepdims=True)
        i1 = jnp.min(jnp.where(el == v1, lanef, big), axis=-1, keepdims=True)
        el2 = jnp.where(lanef == i1, NEG_INF, el)
        v2 = jnp.max(el2, axis=-1, keepdims=True)
        i2 = jnp.min(jnp.where(el2 == v2, lanef, big), axis=-1, keepdims=True)
        t = jnp.exp(v2 - v1)
        w1 = gp / (1.0 + t)
        w2 = gp * t / (1.0 + t)
        sel1 = lanef == i1
        sel2 = lanef == i2
        oh = jnp.where(sel1 | sel2, 1.0, 0.0)
        r_i = lax.broadcasted_iota(I32, (tm, tm), 0)
        c_i = lax.broadcasted_iota(I32, (tm, tm), 1)
        lower = jnp.where(c_i < r_i, 1.0, 0.0).astype(BF16)
        before = _dot(lower, oh.astype(BF16)) + run_ref[...]
        rank1 = jnp.sum(jnp.where(sel1, before, 0.0), axis=-1, keepdims=True)
        rank2 = jnp.sum(jnp.where(sel2, before, 0.0), axis=-1, keepdims=True)
        run_ref[...] = run_ref[...] + jnp.sum(oh, axis=0, keepdims=True)
        cnt_ref[...] = run_ref[...]
        route = jnp.where(lane == 0, i1, jnp.where(lane == 1, i2, jnp.where(lane == 2, w1, jnp.where(
            lane == 3, w2, jnp.where(lane == 4, rank1, jnp.where(lane == 5, rank2, 0.0))))))
        route_ref[...] = route

    step()


PACK_SUB = SUBLANES


def _pack_rows(v, o_ref):
    tm = v.shape[0]
    half = PACK_SUB * LANES
    bits = pltpu.bitcast(v.astype(BF16).astype(F32), jnp.uint32)
    for c in range(PACK_SUB):
        lo = lax.shift_right_logical(bits[:, c * LANES:(c + 1) * LANES], jnp.uint32(16))
        hi = bits[:, half + c * LANES:half + (c + 1) * LANES] & jnp.uint32(0xFFFF0000)
        o_ref[pl.ds(c, tm, stride=PACK_SUB), :] = lo | hi


def _unpack_chunk(src_ref, c, rows):
    w = src_ref[pl.ds(c, rows, stride=PACK_SUB), :]
    lo = pltpu.bitcast(lax.shift_left(w, jnp.uint32(16)), F32)
    hi = pltpu.bitcast(w & jnp.uint32(0xFFFF0000), F32)
    return lo, hi


def _expert_kernel(blk_e_ref, d0_ref, d1_ref, nxt_e_ref, nused_ref, h2_hbm, wg_hbm, wu_hbm, wd_hbm, ys_ref,
                   row_tok_ref, xbuf, xb, wg_stg, wu_stg, wd_stg, wgb, wub, wdb, sem, wsem, *, layer):
    b = pl.program_id(0)
    nused = nused_ref[0]
    slot = lax.rem(b, 2)

    @pl.when(b == 0)
    def _():
        def clear(r, carry):
            row_tok_ref[r] = 0
            return carry

        def fill(t, carry):
            row_tok_ref[d0_ref[t]] = t
            row_tok_ref[d1_ref[t]] = t
            return carry

        lax.fori_loop(0, row_tok_ref.shape[0], clear, 0, unroll=SCALAR_UNROLL)
        lax.fori_loop(0, d0_ref.shape[0], fill, 0, unroll=SCALAR_UNROLL)

    def gather_start(blk, sl):
        def body(r, carry):
            src = pl.multiple_of(row_tok_ref[blk * MOE_TM + r] * PACK_SUB, PACK_SUB)
            dst = pl.multiple_of(r * PACK_SUB, PACK_SUB)
            pltpu.make_async_copy(h2_hbm.at[pl.ds(src, PACK_SUB), :], xbuf.at[sl, pl.ds(dst, PACK_SUB), :],
                                  sem.at[sl]).start()
            return carry

        lax.fori_loop(0, MOE_TM, body, 0, unroll=GATHER_UNROLL)

    def gather_wait(sl):
        pltpu.make_async_copy(h2_hbm.at[pl.ds(0, MOE_TM * PACK_SUB), :], xbuf.at[sl], sem.at[sl]).wait()

    def weight_copies(e):
        return (pltpu.make_async_copy(wg_hbm.at[layer, e], wg_stg, wsem.at[0]),
                pltpu.make_async_copy(wu_hbm.at[layer, e], wu_stg, wsem.at[1]),
                pltpu.make_async_copy(wd_hbm.at[layer, e], wd_stg, wsem.at[2]))

    @pl.when((b == 0) & (nused > 0))
    def _():
        gather_start(0, 0)
        for cp in weight_copies(blk_e_ref[0]):
            cp.start(priority=WEIGHT_DMA_PRIORITY)

    @pl.when(b + 1 < nused)
    def _():
        gather_start(b + 1, 1 - slot)

    @pl.when(b < nused)
    def _():
        e = blk_e_ref[b]
        prev = blk_e_ref[jnp.maximum(b - 1, 0)]

        @pl.when((b == 0) | (e != prev))
        def _():
            for cp in weight_copies(e):
                cp.wait()
            wgb[...] = wg_stg[...].astype(BF16)
            wub[...] = wu_stg[...].astype(BF16)
            wdb[...] = wd_stg[...].astype(BF16)
            nxt = nxt_e_ref[e]

            @pl.when(nxt >= 0)
            def _():
                for cp in weight_copies(nxt):
                    cp.start(priority=WEIGHT_DMA_PRIORITY)

        gather_wait(slot)
        half = PACK_SUB * LANES
        for c in range(PACK_SUB):
            lo, hi = _unpack_chunk(xbuf.at[slot], c, MOE_TM)
            xb[:, c * LANES:(c + 1) * LANES] = lo.astype(BF16)
            xb[:, half + c * LANES:half + (c + 1) * LANES] = hi.astype(BF16)
        x = xb[...]
        hid = _silu(_dot(x, wgb[...])) * _dot(x, wub[...])
        _pack_rows(_dot(hid.astype(BF16), wdb[...]), ys_ref)

    @pl.when(b >= nused)
    def _():
        ys_ref[...] = jnp.zeros_like(ys_ref)


def _combine_kernel(d0_ref, d1_ref, ys_hbm, x1_ref, route_ref, g2_ref, o_ref, ybuf, sem):
    i = pl.program_id(0)
    n = pl.num_programs(0)
    slot = lax.rem(i, 2)

    def gather_start(tile, sl):
        def body(r, carry):
            t = tile * CMB_TT + r
            dst = pl.multiple_of(r * PACK_SUB, PACK_SUB)
            for j, d_ref in enumerate((d0_ref, d1_ref)):
                src = pl.multiple_of(d_ref[t] * PACK_SUB, PACK_SUB)
                pltpu.make_async_copy(ys_hbm.at[pl.ds(src, PACK_SUB), :], ybuf.at[sl, j, pl.ds(dst, PACK_SUB), :],
                                      sem.at[sl]).start()
            return carry

        lax.fori_loop(0, CMB_TT, body, 0, unroll=GATHER_UNROLL)

    def gather_wait(sl):
        for j in range(EXPERT_TOPK):
            pltpu.make_async_copy(ys_hbm.at[pl.ds(0, CMB_TT * PACK_SUB), :], ybuf.at[sl, j], sem.at[sl]).wait()

    @pl.when(i == 0)
    def _():
        gather_start(0, 0)

    @pl.when(i + 1 < n)
    def _():
        gather_start(i + 1, 1 - slot)

    gather_wait(slot)
    w0 = route_ref[:, 2:3]
    w1 = route_ref[:, 3:4]
    half = PACK_SUB * LANES
    for c in range(PACK_SUB):
        lo0, hi0 = _unpack_chunk(ybuf.at[slot, 0], c, CMB_TT)
        lo1, hi1 = _unpack_chunk(ybuf.at[slot, 1], c, CMB_TT)
        for off, y0, y1 in ((c * LANES, lo0, lo1), (half + c * LANES, hi0, hi1)):
            cols = slice(off, off + LANES)
            o_ref[:, cols] = x1_ref[:, cols] + g2_ref[:, cols] * (y0 * w0 + y1 * w1)


def _rope_tables(seq, dh):
    nf = dh // 4
    t = jnp.arange(seq)
    inv = ROPE_BASE ** (-jnp.arange(nf, dtype=F32) / nf)
    ang_r = (t // GRID_W).astype(F32)[:, None] * inv[None, :]
    ang_c = (t % GRID_W).astype(F32)[:, None] * inv[None, :]
    cos = jnp.concatenate([jnp.cos(ang_r)] * 2 + [jnp.cos(ang_c)] * 2, axis=-1)
    sin = jnp.concatenate([-jnp.sin(ang_r), jnp.sin(ang_r), -jnp.sin(ang_c), jnp.sin(ang_c)], axis=-1)
    return cos, sin


def kernel(x_prompt, x_sample, cache_attn_k, cache_attn_v, state_retention, c, c_ctx, w_mod, b_mod, norm1_g, norm2_g, w_in, conv_dw_w, conv_dw_b, conv_ln_g, conv_ln_b, ret_decay, q_norm_g, k_norm_g, attn_sink, w_out, router_group_w, router_group_b, router_expert_w, router_expert_b, expert_w_gate, expert_w_up, expert_w_down):
    batch, ctx_seq, d = x_prompt.shape
    dec_batch, lat_seq, _ = x_sample.shape
    depth = w_mod.shape[0]
    in_width = w_in.shape[2]
    conv_width, conv_ch = conv_dw_w.shape[1:]
    ret_heads = ret_decay.shape[2]
    ret_hd = state_retention.shape[-1]
    att_heads = attn_sink.shape[1]
    kv_heads, att_hd = cache_attn_k.shape[3:]
    past_len = cache_attn_k.shape[2]
    group = att_heads // kv_heads
    n_groups = router_group_w.shape[2]
    n_experts = router_expert_w.shape[2]
    d_expert = expert_w_gate.shape[3]
    nctx, nlat = batch * ctx_seq, dec_batch * lat_seq
    nt = nctx + nlat
    ret_w = ret_heads * ret_hd
    att_w = att_heads * att_hd
    kv_w = kv_heads * att_hd
    c_ca, c_cg = 0, conv_ch
    c_rq = 2 * conv_ch
    c_rk, c_rv, c_rg = c_rq + ret_w, c_rq + 2 * ret_w, c_rq + 3 * ret_w
    c_aq = c_rq + 4 * ret_w
    c_ak = c_aq + att_w
    c_av = c_ak + kv_w
    assert c_av + kv_w == in_width
    assert ctx_seq == CONV_TT == RET_TQ and lat_seq % ATT_TQ == 0 and INPROJ_TM == lat_seq
    assert conv_width // 2 <= CONV_HALO and n_experts + n_groups <= LANES
    assert d == 2 * PACK_SUB * LANES

    mod_rows = SUBLANES
    assert 1 + dec_batch <= mod_rows
    cv = jnp.concatenate([c_ctx[None, :], c, jnp.zeros((mod_rows - 1 - dec_batch, d), F32)], axis=0)
    mods = _modulation(cv, w_mod, b_mod).reshape(depth, mod_rows, 1, 6 * d)

    def mod_row(t0):
        return jnp.where(t0 < nctx, 0, (t0 - nctx) // lat_seq + 1)

    lg = jax.nn.log_sigmoid(ret_decay.astype(F32)).reshape(-1)
    sinks = attn_sink.astype(F32).reshape(-1)
    cos_t, sin_t = _rope_tables(lat_seq, att_hd)
    rw = jnp.concatenate([router_expert_w, router_group_w, jnp.zeros((depth, d, LANES - n_experts - n_groups), F32)], axis=2)
    rb = jnp.concatenate([router_expert_b, router_group_b, jnp.zeros((depth, LANES - n_experts - n_groups), F32)], axis=1)
    rb = rb.reshape(depth, 1, LANES)
    kc_all = cache_attn_k.reshape(dec_batch, depth, past_len, kv_w)
    vc_all = cache_attn_v.reshape(dec_batch, depth, past_len, kv_w)
    n_assign = nt * EXPERT_TOPK
    p_rows = n_assign + n_experts * MOE_TM
    n_blk = p_rows // MOE_TM
    smem = pl.BlockSpec(memory_space=pltpu.SMEM)
    hbm = pl.BlockSpec(memory_space=pl.ANY)

    x = jnp.concatenate([x_prompt.reshape(nctx, d), x_sample.reshape(nlat, d)], axis=0)
    k_list, v_list, s_list = [], [], []
    for l in range(depth):
        n_j = in_width // INPROJ_TN
        proj = pl.pallas_call(
            _inproj_kernel,
            out_shape=jax.ShapeDtypeStruct((nt, in_width), BF16),
            grid=(nt // INPROJ_TM, n_j),
            in_specs=[
                pl.BlockSpec((INPROJ_TM, d), lambda i, j: (i, 0)),
                pl.BlockSpec((None, None, 1, d), lambda i, j, l=l: (l, mod_row(i * INPROJ_TM), 0, 1)),
                pl.BlockSpec((None, None, 1, d), lambda i, j, l=l: (l, mod_row(i * INPROJ_TM), 0, 0)),
                pl.BlockSpec((None, 1, d), lambda i, j, l=l: (l, 0, 0)),
                pl.BlockSpec((None, d, INPROJ_TN), lambda i, j, l=l: (l, 0, jnp.where(i == 0, j, n_j - 1))),
            ],
            out_specs=pl.BlockSpec((INPROJ_TM, INPROJ_TN), lambda i, j: (i, j)),
            scratch_shapes=[pltpu.VMEM((INPROJ_TM, d), BF16), pltpu.VMEM((n_j, d, INPROJ_TN), BF16)],
            compiler_params=_cparams(("arbitrary", "arbitrary")),
            name="inproj",
        )(x, mods, mods, norm1_g.reshape(depth, 1, d), w_in)

        hb = CONV_TT // CONV_HALO
        n_hb = nt // CONV_HALO
        conv_out = pl.pallas_call(
            functools.partial(_conv_kernel, nctx=nctx, ctx_seq=ctx_seq, lat_seq=lat_seq, width=conv_width),
            out_shape=jax.ShapeDtypeStruct((nt, conv_ch), BF16),
            grid=(nt // CONV_TT,),
            in_specs=[
                pl.BlockSpec((CONV_TT, conv_ch), lambda i: (i, c_ca // conv_ch)),
                pl.BlockSpec((CONV_TT, conv_ch), lambda i: (i, c_cg // conv_ch)),
                pl.BlockSpec((CONV_HALO, conv_ch), lambda i: (jnp.maximum(i * hb - 1, 0), c_ca // conv_ch)),
                pl.BlockSpec((CONV_HALO, conv_ch), lambda i: (jnp.maximum(i * hb - 1, 0), c_cg // conv_ch)),
                pl.BlockSpec((CONV_HALO, conv_ch), lambda i: (jnp.minimum((i + 1) * hb, n_hb - 1), c_ca // conv_ch)),
                pl.BlockSpec((CONV_HALO, conv_ch), lambda i: (jnp.minimum((i + 1) * hb, n_hb - 1), c_cg // conv_ch)),
                pl.BlockSpec((None, conv_width, conv_ch), lambda i, l=l: (l, 0, 0)),
                pl.BlockSpec((None, 1, conv_ch), lambda i, l=l: (l, 0, 0)),
                pl.BlockSpec((None, 1, conv_ch), lambda i, l=l: (l, 0, 0)),
                pl.BlockSpec((None, 1, conv_ch), lambda i, l=l: (l, 0, 0)),
            ],
            out_specs=pl.BlockSpec((CONV_TT, conv_ch), lambda i: (i, 0)),
            scratch_shapes=[pltpu.VMEM((SUBLANES, CONV_TT + 2 * CONV_HALO, conv_ch), F32)],
            compiler_params=_cparams(("arbitrary",)),
            name="conv",
        )(proj, proj, proj, proj, proj, proj, conv_dw_w, conv_dw_b.reshape(depth, 1, conv_ch),
          conv_ln_g.reshape(depth, 1, conv_ch), conv_ln_b.reshape(depth, 1, conv_ch))

        def ret_call(nseq, seq, row0, has_s0, emit_state):
            nq = seq // RET_TQ
            in_specs = [
                smem,
                pl.BlockSpec((RET_TQ, ret_w), lambda b, qi: (row0 // RET_TQ + b * nq + qi, c_rq // ret_w)),
                pl.BlockSpec((seq, ret_w), lambda b, qi: (row0 // seq + b, c_rk // ret_w)),
                pl.BlockSpec((seq, ret_w), lambda b, qi: (row0 // seq + b, c_rv // ret_w)),
                pl.BlockSpec((RET_TQ, ret_w), lambda b, qi: (row0 // RET_TQ + b * nq + qi, c_rg // ret_w)),
            ]
            args = [lg, proj, proj, proj, proj]
            if has_s0:
                in_specs.append(pl.BlockSpec((None, None, 2, ret_heads, ret_hd, ret_hd),
                                             lambda b, qi, l=l: (b, l, 0, 0, 0, 0)))
                args.append(state_retention)
            out_shape = [jax.ShapeDtypeStruct((nseq * seq, ret_w), BF16)]
            out_specs = [pl.BlockSpec((RET_TQ, ret_w), lambda b, qi: (b * nq + qi, 0))]
            if emit_state:
                out_shape.append(jax.ShapeDtypeStruct((nseq, 2, ret_heads, ret_hd, ret_hd), F32))
                out_specs.append(pl.BlockSpec((None, 2, ret_heads, ret_hd, ret_hd), lambda b, qi: (b, 0, 0, 0, 0)))
            return pl.pallas_call(
                functools.partial(_ret_kernel, layer=l, heads=ret_heads, seq=seq, has_s0=has_s0, emit_state=emit_state),
                out_shape=out_shape,
                grid=(nseq, nq),
                in_specs=in_specs,
                out_specs=out_specs,
                compiler_params=_cparams(("arbitrary", "arbitrary")),
                name="retention",
            )(*args)

        ret_ctx, s_l = ret_call(batch, ctx_seq, 0, False, True)
        (ret_lat,) = ret_call(dec_batch, lat_seq, nctx, True, False)
        s_list.append(s_l)

        gq = q_norm_g.reshape(depth, 1, att_hd)
        gk = k_norm_g.reshape(depth, 1, att_hd)
        att_ctx, k_l, v_l = pl.pallas_call(
            functools.partial(_attn_ctx_kernel, layer=l, group=group, scale=att_hd ** -0.5, nkv=kv_heads),
            out_shape=[jax.ShapeDtypeStruct((nctx, att_w), BF16),
                       jax.ShapeDtypeStruct((batch, ctx_seq, kv_w), F32),
                       jax.ShapeDtypeStruct((batch, ctx_seq, kv_w), F32)],
            grid=(batch,),
            in_specs=[
                smem,
                pl.BlockSpec((ctx_seq, att_w), lambda b: (b, c_aq // att_w)),
                pl.BlockSpec((ctx_seq, kv_w), lambda b: (b, c_ak // kv_w)),
                pl.BlockSpec((ctx_seq, kv_w), lambda b: (b, c_av // kv_w)),
                pl.BlockSpec((None, 1, att_hd), lambda b, l=l: (l, 0, 0)),
                pl.BlockSpec((None, 1, att_hd), lambda b, l=l: (l, 0, 0)),
            ],
            out_specs=[
                pl.BlockSpec((ctx_seq, att_w), lambda b: (b, 0)),
                pl.BlockSpec((None, ctx_seq, kv_w), lambda b: (b, 0, 0)),
                pl.BlockSpec((None, ctx_seq, kv_w), lambda b: (b, 0, 0)),
            ],
            compiler_params=_cparams(("arbitrary",)),
            name="attn_ctx",
        )(sinks, proj, proj, proj, gq, gk)
        k_list.append(k_l)
        v_list.append(v_l)

        nq = lat_seq // ATT_TQ
        att_lat = pl.pallas_call(
            functools.partial(_attn_lat_kernel, layer=l, group=group, scale=att_hd ** -0.5, seq=lat_seq, nkv=kv_heads),
            out_shape=jax.ShapeDtypeStruct((nlat, att_w), BF16),
            grid=(dec_batch, nq),
            in_specs=[
                smem,
                pl.BlockSpec((ATT_TQ, att_w), lambda b, qi: (nctx // ATT_TQ + b * nq + qi, c_aq // att_w)),
                pl.BlockSpec((lat_seq, kv_w), lambda b, qi: (nctx // lat_seq + b, c_ak // kv_w)),
                pl.BlockSpec((lat_seq, kv_w), lambda b, qi: (nctx // lat_seq + b, c_av // kv_w)),
                pl.BlockSpec((None, None, past_len, kv_w), lambda b, qi, l=l: (b, l, 0, 0)),
                pl.BlockSpec((None, None, past_len, kv_w), lambda b, qi, l=l: (b, l, 0, 0)),
                pl.BlockSpec((None, 1, att_hd), lambda b, qi, l=l: (l, 0, 0)),
                pl.BlockSpec((None, 1, att_hd), lambda b, qi, l=l: (l, 0, 0)),
                pl.BlockSpec((lat_seq, att_hd), lambda b, qi: (0, 0)),
                pl.BlockSpec((lat_seq, att_hd), lambda b, qi: (0, 0)),
            ],
            out_specs=pl.BlockSpec((ATT_TQ, att_w), lambda b, qi: (b * nq + qi, 0)),
            scratch_shapes=[pltpu.VMEM((lat_seq, kv_w), BF16)],
            compiler_params=_cparams(("arbitrary", "arbitrary")),
            name="attn_lat",
        )(sinks, proj, proj, proj, kc_all, vc_all, gq, gk, cos_t, sin_t)

        ctx_tiles = nctx // OUT_TM
        lat_tiles = nlat // OUT_TM

        def ctx_rows(i):
            return jnp.minimum(i, ctx_tiles - 1)

        def lat_rows(i):
            return jnp.maximum(i - ctx_tiles, 0)

        def tile_mod(chunk):
            return pl.BlockSpec((None, None, 1, d), lambda s, l=l: (l, mod_row(s * OUT_TM), 0, chunk))

        x1, h2, route, cnt = pl.pallas_call(
            functools.partial(_outproj_kernel, n_groups=n_groups, n_experts=n_experts, ctx_tiles=ctx_tiles),
            out_shape=[jax.ShapeDtypeStruct((nt, d), F32), jax.ShapeDtypeStruct((nt * PACK_SUB, LANES), jnp.uint32),
                       jax.ShapeDtypeStruct((nt, LANES), F32), jax.ShapeDtypeStruct((1, LANES), F32)],
            grid=(ctx_tiles + lat_tiles,),
            in_specs=[
                pl.BlockSpec((OUT_TM, conv_ch), lambda s: (s, 0)),
                pl.BlockSpec((OUT_TM, ret_w), lambda s: (ctx_rows(s), 0)),
                pl.BlockSpec((OUT_TM, ret_w), lambda s: (lat_rows(s), 0)),
                pl.BlockSpec((OUT_TM, att_w), lambda s: (ctx_rows(s), 0)),
                pl.BlockSpec((OUT_TM, att_w), lambda s: (lat_rows(s), 0)),
                pl.BlockSpec((OUT_TM, d), lambda s: (s, 0)),
                tile_mod(2),
                tile_mod(4),
                tile_mod(3),
                pl.BlockSpec((None, 1, d), lambda s, l=l: (l, 0, 0)),
                pl.BlockSpec((None, d, d), lambda s, l=l: (l, 0, 0), pipeline_mode=pl.Buffered(1)),
                pl.BlockSpec((None, d, LANES), lambda s, l=l: (l, 0, 0), pipeline_mode=pl.Buffered(1)),
                pl.BlockSpec((None, 1, LANES), lambda s, l=l: (l, 0, 0)),
            ],
            out_specs=[
                pl.BlockSpec((OUT_TM, d), lambda s: (s, 0)),
                pl.BlockSpec((OUT_TM * PACK_SUB, LANES), lambda s: (s, 0)),
                pl.BlockSpec((OUT_TM, LANES), lambda s: (s, 0)),
                pl.BlockSpec((1, LANES), lambda s: (0, 0)),
            ],
            scratch_shapes=[pltpu.VMEM((d, d), BF16), pltpu.VMEM((d, LANES), BF16), pltpu.VMEM((1, LANES), F32)],
            compiler_params=_cparams(("arbitrary",)),
            name="outproj_router",
        )(conv_out, ret_ctx, ret_lat, att_ctx, att_lat, x, mods, mods, mods, norm2_g.reshape(depth, 1, d), w_out, rw, rb)

        eid = route[:, 0:EXPERT_TOPK].astype(I32)
        rank = route[:, 4:4 + EXPERT_TOPK].astype(I32)
        counts = cnt[0, :n_experts].astype(I32)
        padded = ((counts + MOE_TM - 1) // MOE_TM) * MOE_TM
        pends = jnp.cumsum(padded)
        pstarts = pends - padded
        ek = jnp.arange(n_experts, dtype=I32)
        dest = rank + jnp.sum(jnp.where(eid[:, :, None] == ek, pstarts, 0), axis=-1)
        dest0, dest1 = dest[:, 0], dest[:, 1]
        blk_start = jnp.arange(n_blk, dtype=I32) * MOE_TM
        blk_e = jnp.minimum(jnp.sum((pends[None, :] <= blk_start[:, None]).astype(I32), axis=1), n_experts - 1)
        later = (ek[None, :] > ek[:, None]) & (counts[None, :] > 0)
        nxt_e = jnp.min(jnp.where(later, ek[None, :], n_experts), axis=1)
        nxt_e = jnp.where(nxt_e == n_experts, -1, nxt_e).astype(I32)
        nused = (pends[-1:] // MOE_TM).astype(I32)

        ys = pl.pallas_call(
            functools.partial(_expert_kernel, layer=l),
            out_shape=jax.ShapeDtypeStruct((p_rows * PACK_SUB, LANES), jnp.uint32),
            grid_spec=pltpu.PrefetchScalarGridSpec(
                num_scalar_prefetch=5,
                grid=(n_blk,),
                in_specs=[hbm, hbm, hbm, hbm],
                out_specs=pl.BlockSpec((MOE_TM * PACK_SUB, LANES), lambda b, *_: (b, 0)),
                scratch_shapes=[pltpu.SMEM((p_rows,), I32),
                                pltpu.VMEM((2, MOE_TM * PACK_SUB, LANES), jnp.uint32), pltpu.VMEM((MOE_TM, d), BF16),
                                pltpu.VMEM((d, d_expert), F32), pltpu.VMEM((d, d_expert), F32),
                                pltpu.VMEM((d_expert, d), F32),
                                pltpu.VMEM((d, d_expert), BF16), pltpu.VMEM((d, d_expert), BF16),
                                pltpu.VMEM((d_expert, d), BF16),
                                pltpu.SemaphoreType.DMA((2,)), pltpu.SemaphoreType.DMA((3,))],
            ),
            compiler_params=_cparams(("arbitrary",)),
            name="experts",
        )(blk_e, dest0, dest1, nxt_e, nused, h2, expert_w_gate, expert_w_up, expert_w_down)

        x = pl.pallas_call(
            _combine_kernel,
            out_shape=jax.ShapeDtypeStruct((nt, d), F32),
            grid_spec=pltpu.PrefetchScalarGridSpec(
                num_scalar_prefetch=2,
                grid=(nt // CMB_TT,),
                in_specs=[
                    hbm,
                    pl.BlockSpec((CMB_TT, d), lambda i, d0, d1: (i, 0)),
                    pl.BlockSpec((CMB_TT, LANES), lambda i, d0, d1: (i, 0)),
                    pl.BlockSpec((None, None, 1, d), lambda i, d0, d1, l=l: (l, mod_row(i * CMB_TT), 0, 5)),
                ],
                out_specs=pl.BlockSpec((CMB_TT, d), lambda i, d0, d1: (i, 0)),
                scratch_shapes=[pltpu.VMEM((2, EXPERT_TOPK, CMB_TT * PACK_SUB, LANES), jnp.uint32),
                                pltpu.SemaphoreType.DMA((2,))],
            ),
            compiler_params=_cparams(("arbitrary",)),
            name="combine",
        )(dest0, dest1, ys, x1, route, mods)

    y_prompt = x[:nctx].reshape(batch, ctx_seq, d)
    y_sample = x[nctx:].reshape(dec_batch, lat_seq, d)
    new_k = jnp.stack(k_list, axis=1).reshape(batch, depth, ctx_seq, kv_heads, att_hd)
    new_v = jnp.stack(v_list, axis=1).reshape(batch, depth, ctx_seq, kv_heads, att_hd)
    new_s = jnp.stack(s_list, axis=1)
    return (y_prompt, y_sample, new_k, new_v, new_s)
```

```python
import functools

import jax
import jax.numpy as jnp
from jax import lax
from jax.experimental import pallas as pl
from jax.experimental.pallas import tpu as pltpu

F32, BF16, I32 = jnp.float32, jnp.bfloat16, jnp.int32

GRID_W = 64
ATT_WINDOW = 128
ROPE_BASE = 10000.0
EXPERT_TOPK = 2
EPS = 1e-6
NEG_INF = -1e30

LANES = 128
SUBLANES = 8
VMEM_LIMIT_BYTES = 56 * 1024 * 1024

MOD_TN = 1024
INPROJ_TM = 1024
INPROJ_TN = 512
CONV_TT = 256
CONV_HALO = 16
CONV_RC = 32
RET_TQ = 256
ATT_TQ = 256
OUT_TM = 256
MOE_TM = 128
CMB_TT = 256
GATHER_UNROLL = 8
SCALAR_UNROLL = 16
WEIGHT_DMA_PRIORITY = 1


def _cparams(sem):
    return pltpu.CompilerParams(dimension_semantics=sem, vmem_limit_bytes=VMEM_LIMIT_BYTES)


def _rms(x, g):
    return x * lax.rsqrt(jnp.mean(x * x, axis=-1, keepdims=True) + EPS) * g


def _silu(x):
    return x * jax.nn.sigmoid(x)


def _dot(a, b):
    return jnp.dot(a, b, preferred_element_type=F32)


def _dot_nt(a, b):
    return lax.dot_general(a, b, (((1,), (1,)), ((), ())), preferred_element_type=F32)


def _dot_tn(a, b):
    return lax.dot_general(a, b, (((0,), (0,)), ((), ())), preferred_element_type=F32)


def _mod_kernel(cv_ref, w_ref, b_ref, o_ref):
    s = _silu(cv_ref[...]).astype(BF16)
    o_ref[...] = _dot(s, w_ref[...].astype(BF16)) + b_ref[...]


def _modulation(cv, w_mod, b_mod):
    depth, d, n6 = w_mod.shape
    rows = cv.shape[0]
    return pl.pallas_call(
        _mod_kernel,
        out_shape=jax.ShapeDtypeStruct((depth, rows, n6), F32),
        grid=(depth, n6 // MOD_TN),
        in_specs=[
            pl.BlockSpec((rows, d), lambda l, j: (0, 0)),
            pl.BlockSpec((None, d, MOD_TN), lambda l, j: (l, 0, j)),
            pl.BlockSpec((None, 1, MOD_TN), lambda l, j: (l, 0, j)),
        ],
        out_specs=pl.BlockSpec((None, rows, MOD_TN), lambda l, j: (l, 0, j)),
        compiler_params=_cparams(("arbitrary", "arbitrary")),
        name="modulation",
    )(cv, w_mod, b_mod.reshape(depth, 1, n6))


def _inproj_kernel(x_ref, sc_ref, sh_ref, g_ref, w_ref, o_ref, h_ref, wb_ref):
    i = pl.program_id(0)
    j = pl.program_id(1)

    @pl.when(i == 0)
    def _():
        wb_ref[j] = w_ref[...].astype(BF16)

    @pl.when(j == 0)
    def _():
        h = _rms(x_ref[...], g_ref[...]) * (1.0 + sc_ref[...]) + sh_ref[...]
        h_ref[...] = h.astype(BF16)

    o_ref[...] = _dot(h_ref[...], wb_ref[j]).astype(BF16)


def _conv_kernel(a_ref, g_ref, ap_ref, gp_ref, an_ref, gn_ref, w_ref, b_ref, lng_ref, lnb_ref, o_ref,
                 hs_ref, *, nctx, ctx_seq, lat_seq, width):
    t0 = pl.program_id(0) * CONV_TT
    is_ctx = t0 < nctx
    seq_len = jnp.where(is_ctx, ctx_seq, lat_seq)
    pos = lax.rem(jnp.where(is_ctx, t0, t0 - nctx), seq_len)
    first = pos == 0
    last = pos + CONV_TT == seq_len
    rows = CONV_TT + 2 * CONV_HALO
    span = rows - SUBLANES
    def glu(a, g):
        return a[...].astype(F32) * jax.nn.sigmoid(g[...].astype(F32))

    hs_ref[0, 0:CONV_HALO, :] = jnp.where(first, 0.0, glu(ap_ref, gp_ref))
    hs_ref[0, CONV_HALO:CONV_HALO + CONV_TT, :] = glu(a_ref, g_ref)
    hs_ref[0, CONV_HALO + CONV_TT:rows, :] = jnp.where(last, 0.0, glu(an_ref, gn_ref))
    for s in range(1, SUBLANES):
        hs_ref[s, 0:span, :] = hs_ref[0, s:s + span, :]
    base = CONV_HALO - width // 2
    for r0 in range(0, CONV_TT, CONV_RC):
        acc = jnp.zeros((CONV_RC, o_ref.shape[1]), F32)
        for k in range(width):
            idx = k + base
            s, q = idx % SUBLANES, idx // SUBLANES
            acc = acc + w_ref[k:k + 1, :] * hs_ref[s, r0 + SUBLANES * q:r0 + SUBLANES * q + CONV_RC, :]
        hf = acc + b_ref[...]
        mu = jnp.mean(hf, axis=-1, keepdims=True)
        var = jnp.mean(jnp.square(hf - mu), axis=-1, keepdims=True)
        y = (hf - mu) * lax.rsqrt(var + EPS) * lng_ref[...] + lnb_ref[...]
        o_ref[r0:r0 + CONV_RC, :] = _silu(y).astype(BF16)


def _ret_kernel(lg_ref, q_ref, k_ref, v_ref, gate_ref, *rest, layer, heads, seq, has_s0, emit_state):
    rest = list(rest)
    s0_ref = rest.pop(0) if has_s0 else None
    o_ref = rest.pop(0)
    st_ref = rest.pop(0) if emit_state else None
    qi = pl.program_id(1)
    dh = q_ref.shape[1] // heads
    n_idx = qi * RET_TQ + lax.broadcasted_iota(I32, (RET_TQ, seq), 0)
    m_idx = lax.broadcasted_iota(I32, (RET_TQ, seq), 1)
    rel = (n_idx - m_idx).astype(F32)
    npos = (qi * RET_TQ + lax.broadcasted_iota(I32, (RET_TQ, 1), 0)).astype(F32)
    mpos = lax.broadcasted_iota(I32, (seq, 1), 0).astype(F32)
    for h in range(heads):
        cols = slice(h * dh, (h + 1) * dh)
        lgf = lg_ref[layer * 2 * heads + h]
        lgb = lg_ref[layer * 2 * heads + heads + h]
        q = q_ref[:, cols].astype(F32)
        k = k_ref[:, cols].astype(F32) * (dh ** -0.5)
        vb = v_ref[:, cols]
        a = _dot_nt(q_ref[:, cols], k.astype(BF16))
        decay = jnp.exp(jnp.where(rel >= 0.0, lgf, -lgb) * rel)
        o = _dot((a * decay).astype(BF16), vb)
        if has_s0:
            cf = jnp.exp(lgf * (npos + 1.0))
            cb = jnp.exp(lgb * (seq - npos))
            o = (o + _dot((q * cf).astype(BF16), s0_ref[0, h].astype(BF16))
                 + _dot((q * cb).astype(BF16), s0_ref[1, h].astype(BF16)))
        y = o * lax.rsqrt(jnp.mean(o * o, axis=-1, keepdims=True) + EPS)
        o_ref[:, cols] = (_silu(gate_ref[:, cols].astype(F32)) * y).astype(BF16)
        if emit_state:
            st_ref[0, h] = _dot_tn((k * jnp.exp(lgf * (seq - 1.0 - mpos))).astype(BF16), vb)
            st_ref[1, h] = _dot_tn((k * jnp.exp(lgb * mpos)).astype(BF16), vb)


def _attn_ctx_kernel(sink_ref, q_ref, k_ref, v_ref, gq_ref, gk_ref, o_ref, ko_ref, vo_ref, *, layer, group, scale, nkv):
    dh = k_ref.shape[1] // nkv
    for kv in range(nkv):
        kcols = slice(kv * dh, (kv + 1) * dh)
        kn = _rms(k_ref[:, kcols].astype(F32), gk_ref[...])
        vb = v_ref[:, kcols]
        ko_ref[:, kcols] = kn
        vo_ref[:, kcols] = vb.astype(F32)
        knb = kn.astype(BF16)
        for g in range(group):
            head = kv * group + g
            cols = slice(head * dh, (head + 1) * dh)
            sink = sink_ref[layer * nkv * group + head]
            qn = _rms(q_ref[:, cols].astype(F32), gq_ref[...])
            s = _dot_nt(qn.astype(BF16), knb) * scale
            m = jnp.maximum(jnp.max(s, axis=-1, keepdims=True), sink)
            p = jnp.exp(s - m)
            den = jnp.sum(p, axis=-1, keepdims=True) + jnp.exp(sink - m)
            o_ref[:, cols] = (_dot(p.astype(BF16), vb) / den).astype(BF16)


def _rope(x, cos, sin_signed):
    lane = lax.broadcasted_iota(I32, x.shape, 1)
    half = x.shape[1] // 4
    swapped = jnp.where(lax.rem(lane, 2 * half) < half, pltpu.roll(x, x.shape[1] - half, 1), pltpu.roll(x, half, 1))
    return x * cos + swapped * sin_signed


def _attn_lat_kernel(sink_ref, q_ref, k_ref, v_ref, kc_ref, vc_ref, gq_ref, gk_ref, cos_ref, sin_ref,
                     o_ref, kr_ref, *, layer, group, scale, seq, nkv):
    qi = pl.program_id(1)
    dh = k_ref.shape[1] // nkv
    win = ATT_TQ + 2 * ATT_WINDOW

    @pl.when(qi == 0)
    def _():
        for kv in range(nkv):
            kcols = slice(kv * dh, (kv + 1) * dh)
            kn = _rms(k_ref[:, kcols].astype(F32), gk_ref[...])
            kr_ref[:, kcols] = _rope(kn, cos_ref[...], sin_ref[...]).astype(BF16)

    q0 = pl.multiple_of(qi * ATT_TQ, ATT_TQ)
    start = pl.multiple_of(jnp.clip(qi * ATT_TQ - ATT_WINDOW, 0, seq - win), ATT_WINDOW)
    qpos = q0 + lax.broadcasted_iota(I32, (ATT_TQ, win), 0)
    kpos = start + lax.broadcasted_iota(I32, (ATT_TQ, win), 1)
    valid = jnp.abs(qpos - kpos) <= ATT_WINDOW
    cs = cos_ref[pl.ds(q0, ATT_TQ), :]
    sn = sin_ref[pl.ds(q0, ATT_TQ), :]
    for kv in range(nkv):
        kcols = slice(kv * dh, (kv + 1) * dh)
        kw = kr_ref[pl.ds(start, win), kcols]
        vw = v_ref[pl.ds(start, win), kcols]
        kcb = kc_ref[:, kcols].astype(BF16)
        vcb = vc_ref[:, kcols].astype(BF16)
        for g in range(group):
            head = kv * group + g
            cols = slice(head * dh, (head + 1) * dh)
            sink = sink_ref[layer * nkv * group + head]
            qr = _rope(_rms(q_ref[:, cols].astype(F32), gq_ref[...]), cs, sn).astype(BF16)
            s_loc = jnp.where(valid, _dot_nt(qr, kw) * scale, NEG_INF)
            s_ctx = _dot_nt(qr, kcb) * scale
            m = jnp.maximum(jnp.maximum(jnp.max(s_loc, axis=-1, keepdims=True),
                                        jnp.max(s_ctx, axis=-1, keepdims=True)), sink)
            p_loc = jnp.exp(s_loc - m)
            p_ctx = jnp.exp(s_ctx - m)
            den = jnp.sum(p_loc, axis=-1, keepdims=True) + jnp.sum(p_ctx, axis=-1, keepdims=True) + jnp.exp(sink - m)
            o = _dot(p_loc.astype(BF16), vw) + _dot(p_ctx.astype(BF16), vcb)
            o_ref[:, cols] = (o / den).astype(BF16)


def _outproj_kernel(conv_ref, retc_ref, retl_ref, attc_ref, attl_ref, x_ref, g1_ref, sc2_ref, sh2_ref, n2_ref, wout_ref,
                    rw_ref, rb_ref, x1_ref, h2_ref, route_ref, cnt_ref, wb_ref, rwb_ref, run_ref,
                    *, n_groups, n_experts, ctx_tiles):
    s = pl.program_id(0)
    c_conv = conv_ref.shape[1]
    c_ret = retc_ref.shape[1]
    is_ctx = s < ctx_tiles
    ret = jnp.where(is_ctx, retc_ref[...], retl_ref[...])
    att = jnp.where(is_ctx, attc_ref[...], attl_ref[...])

    @pl.when(s == 0)
    def _():
        wb_ref[...] = wout_ref[...].astype(BF16)
        rwb_ref[...] = rw_ref[...].astype(BF16)
        run_ref[...] = jnp.zeros_like(run_ref)

    def step():
        mix = (_dot(conv_ref[...], wb_ref[0:c_conv, :]) + _dot(ret, wb_ref[c_conv:c_conv + c_ret, :])
               + _dot(att, wb_ref[c_conv + c_ret:, :]))
        x1 = x_ref[...] + g1_ref[...] * mix
        x1_ref[...] = x1
        h2 = _rms(x1, n2_ref[...]) * (1.0 + sc2_ref[...]) + sh2_ref[...]
        _pack_rows(h2, h2_ref)
        lg = _dot(h2.astype(BF16), rwb_ref[...]) + rb_ref[...]
        tm = lg.shape[0]
        lane = lax.broadcasted_iota(I32, lg.shape, 1)
        lanef = lane.astype(F32)
        big = float(LANES)
        gmask = (lane >= n_experts) & (lane < n_experts + n_groups)
        gl = jnp.where(gmask, lg, NEG_INF)
        gmax = jnp.max(gl, axis=-1, keepdims=True)
        gidx = jnp.min(jnp.where(gl == gmax, lanef, big), axis=-1, keepdims=True) - float(n_experts)
        gp = 1.0 / jnp.sum(jnp.where(gmask, jnp.exp(gl - gmax), 0.0), axis=-1, keepdims=True)
        per_group = n_experts // n_groups
        emask = (lanef >= gidx * per_group) & (lanef < (gidx + 1.0) * per_group)
        el = jnp.where(emask, lg, NEG_INF)
        v1 = jnp.max(el, axis=-1, keepdims=True)
        i1 = jnp.min(jnp.where(el == v1, lanef, big), axis=-1, keepdims=True)
        el2 = jnp.where(lanef == i1, NEG_INF, el)
        v2 = jnp.max(el2, axis=-1, keepdims=True)
        i2 = jnp.min(jnp.where(el2 == v2, lanef, big), axis=-1, keepdims=True)
        t = jnp.exp(v2 - v1)
        w1 = gp / (1.0 + t)
        w2 = gp * t / (1.0 + t)
        sel1 = lanef == i1
        sel2 = lanef == i2
        oh = jnp.where(sel1 | sel2, 1.0, 0.0)
        r_i = lax.broadcasted_iota(I32, (tm, tm), 0)
        c_i = lax.broadcasted_iota(I32, (tm, tm), 1)
        lower = jnp.where(c_i < r_i, 1.0, 0.0).astype(BF16)
        before = _dot(lower, oh.astype(BF16)) + run_ref[...]
        rank1 = jnp.sum(jnp.where(sel1, before, 0.0), axis=-1, keepdims=True)
        rank2 = jnp.sum(jnp.where(sel2, before, 0.0), axis=-1, keepdims=True)
        run_ref[...] = run_ref[...] + jnp.sum(oh, axis=0, keepdims=True)
        cnt_ref[...] = run_ref[...]
        route = jnp.where(lane == 0, i1, jnp.where(lane == 1, i2, jnp.where(lane == 2, w1, jnp.where(
            lane == 3, w2, jnp.where(lane == 4, rank1, jnp.where(lane == 5, rank2, 0.0))))))
        route_ref[...] = route

    step()


PACK_SUB = SUBLANES


def _pack_cols(lo_v, hi_v, o_ref, c0):
    tm = lo_v.shape[0]
    lo_bits = pltpu.bitcast(lo_v.astype(BF16).astype(F32), jnp.uint32)
    hi_bits = pltpu.bitcast(hi_v.astype(BF16).astype(F32), jnp.uint32)
    for j in range(lo_v.shape[1] // LANES):
        lo = lax.shift_right_logical(lo_bits[:, j * LANES:(j + 1) * LANES], jnp.uint32(16))
        hi = hi_bits[:, j * LANES:(j + 1) * LANES] & jnp.uint32(0xFFFF0000)
        o_ref[pl.ds(c0 + j, tm, stride=PACK_SUB), :] = lo | hi


def _pack_rows(v, o_ref):
    half = PACK_SUB * LANES
    _pack_cols(v[:, :half], v[:, half:], o_ref, 0)


def _unpack_chunk(src_ref, c, rows):
    w = src_ref[pl.ds(c, rows, stride=PACK_SUB), :]
    lo = pltpu.bitcast(lax.shift_left(w, jnp.uint32(16)), F32)
    hi = pltpu.bitcast(w & jnp.uint32(0xFFFF0000), F32)
    return lo, hi


def _expert_kernel(blk_e_ref, d0_ref, d1_ref, nxt_e_ref, nused_ref, h2_hbm, wg_hbm, wu_hbm, wd_hbm, ys_ref,
                   row_tok_ref, xbuf, xb, hb, wg_stg, wu_stg, wd_stg, wgb, wub, wdb, sem, wsem, *, layer):
    b = pl.program_id(0)
    nused = nused_ref[0]
    slot = lax.rem(b, 2)

    @pl.when(b == 0)
    def _():
        def clear(r, carry):
            row_tok_ref[r] = 0
            return carry

        def fill(t, carry):
            row_tok_ref[d0_ref[t]] = t
            row_tok_ref[d1_ref[t]] = t
            return carry

        lax.fori_loop(0, row_tok_ref.shape[0], clear, 0, unroll=SCALAR_UNROLL)
        lax.fori_loop(0, d0_ref.shape[0], fill, 0, unroll=SCALAR_UNROLL)

    def gather_start(blk, sl):
        def body(r, carry):
            src = pl.multiple_of(row_tok_ref[blk * MOE_TM + r] * PACK_SUB, PACK_SUB)
            dst = pl.multiple_of(r * PACK_SUB, PACK_SUB)
            pltpu.make_async_copy(h2_hbm.at[pl.ds(src, PACK_SUB), :], xbuf.at[sl, pl.ds(dst, PACK_SUB), :],
                                  sem.at[sl]).start()
            return carry

        lax.fori_loop(0, MOE_TM, body, 0, unroll=GATHER_UNROLL)

    def gather_wait(sl):
        pltpu.make_async_copy(h2_hbm.at[pl.ds(0, MOE_TM * PACK_SUB), :], xbuf.at[sl], sem.at[sl]).wait()

    def weight_copies(e):
        return (pltpu.make_async_copy(wg_hbm.at[layer, e], wg_stg, wsem.at[0]),
                pltpu.make_async_copy(wu_hbm.at[layer, e], wu_stg, wsem.at[1]),
                pltpu.make_async_copy(wd_hbm.at[layer, e], wd_stg, wsem.at[2]))

    @pl.when((b == 0) & (nused > 0))
    def _():
        gather_start(0, 0)
        for cp in weight_copies(blk_e_ref[0]):
            cp.start(priority=WEIGHT_DMA_PRIORITY)

    @pl.when(b == nused)
    def _():
        gather_wait(slot)

    @pl.when(b < nused)
    def _():
        e = blk_e_ref[b]
        prev = blk_e_ref[jnp.maximum(b - 1, 0)]

        @pl.when((b == 0) | (e != prev))
        def _():
            for cp in weight_copies(e):
                cp.wait()
            wgb[...] = wg_stg[...].astype(BF16)
            wub[...] = wu_stg[...].astype(BF16)
            wdb[...] = wd_stg[...].astype(BF16)
            nxt = nxt_e_ref[e]

            @pl.when(nxt >= 0)
            def _():
                for cp in weight_copies(nxt):
                    cp.start(priority=WEIGHT_DMA_PRIORITY)

        gather_wait(slot)
        n_pieces = 2 * PACK_SUB
        per_piece = MOE_TM // n_pieces
        issued = [0]

        def issue_some():
            for r in range(issued[0], issued[0] + per_piece):
                src = pl.multiple_of(row_tok_ref[(b + 1) * MOE_TM + r] * PACK_SUB, PACK_SUB)
                pltpu.make_async_copy(h2_hbm.at[pl.ds(src, PACK_SUB), :],
                                      xbuf.at[1 - slot, pl.ds(r * PACK_SUB, PACK_SUB), :], sem.at[1 - slot]).start()
            issued[0] += per_piece

        half = PACK_SUB * LANES
        for c in range(PACK_SUB):
            issue_some()
            lo, hi = _unpack_chunk(xbuf.at[slot], c, MOE_TM)
            xb[:, c * LANES:(c + 1) * LANES] = lo.astype(BF16)
            xb[:, half + c * LANES:half + (c + 1) * LANES] = hi.astype(BF16)
        x = xb[...]
        de = wgb.shape[1]
        for n in range(2):
            cols = slice(n * de // 2, (n + 1) * de // 2)
            issue_some()
            g = _dot(x, wgb[:, cols])
            issue_some()
            u = _dot(x, wub[:, cols])
            hb[:, cols] = (_silu(g) * u).astype(BF16)
        hid = hb[...]
        q = half // 2
        for n in range(2):
            issue_some()
            y_lo = _dot(hid, wdb[:, n * q:(n + 1) * q])
            issue_some()
            y_hi = _dot(hid, wdb[:, half + n * q:half + (n + 1) * q])
            _pack_cols(y_lo, y_hi, ys_ref, n * (PACK_SUB // 2))
        assert issued[0] == MOE_TM

    @pl.when(b >= nused)
    def _():
        ys_ref[...] = jnp.zeros_like(ys_ref)


def _combine_kernel(d0_ref, d1_ref, ys_hbm, x1_ref, route_ref, g2_ref, o_ref, ybuf, sem):
    i = pl.program_id(0)
    n = pl.num_programs(0)
    slot = lax.rem(i, 2)

    def gather_start(tile, sl):
        def body(r, carry):
            t = tile * CMB_TT + r
            dst = pl.multiple_of(r * PACK_SUB, PACK_SUB)
            for j, d_ref in enumerate((d0_ref, d1_ref)):
                src = pl.multiple_of(d_ref[t] * PACK_SUB, PACK_SUB)
                pltpu.make_async_copy(ys_hbm.at[pl.ds(src, PACK_SUB), :], ybuf.at[sl, j, pl.ds(dst, PACK_SUB), :],
                                      sem.at[sl]).start(priority=j)
            return carry

        lax.fori_loop(0, CMB_TT, body, 0, unroll=GATHER_UNROLL)

    def gather_wait(sl):
        for j in range(EXPERT_TOPK):
            pltpu.make_async_copy(ys_hbm.at[pl.ds(0, CMB_TT * PACK_SUB), :], ybuf.at[sl, j], sem.at[sl]).wait()

    @pl.when(i == 0)
    def _():
        gather_start(0, 0)

    @pl.when(i + 1 < n)
    def _():
        gather_start(i + 1, 1 - slot)

    gather_wait(slot)
    w0 = route_ref[:, 2:3]
    w1 = route_ref[:, 3:4]
    half = PACK_SUB * LANES
    for c in range(PACK_SUB):
        lo0, hi0 = _unpack_chunk(ybuf.at[slot, 0], c, CMB_TT)
        lo1, hi1 = _unpack_chunk(ybuf.at[slot, 1], c, CMB_TT)
        for off, y0, y1 in ((c * LANES, lo0, lo1), (half + c * LANES, hi0, hi1)):
            cols = slice(off, off + LANES)
            o_ref[:, cols] = x1_ref[:, cols] + g2_ref[:, cols] * (y0 * w0 + y1 * w1)


def _rope_tables(seq, dh):
    nf = dh // 4
    t = jnp.arange(seq)
    inv = ROPE_BASE ** (-jnp.arange(nf, dtype=F32) / nf)
    ang_r = (t // GRID_W).astype(F32)[:, None] * inv[None, :]
    ang_c = (t % GRID_W).astype(F32)[:, None] * inv[None, :]
    cos = jnp.concatenate([jnp.cos(ang_r)] * 2 + [jnp.cos(ang_c)] * 2, axis=-1)
    sin = jnp.concatenate([-jnp.sin(ang_r), jnp.sin(ang_r), -jnp.sin(ang_c), jnp.sin(ang_c)], axis=-1)
    return cos, sin


def kernel(x_prompt, x_sample, cache_attn_k, cache_attn_v, state_retention, c, c_ctx, w_mod, b_mod, norm1_g, norm2_g, w_in, conv_dw_w, conv_dw_b, conv_ln_g, conv_ln_b, ret_decay, q_norm_g, k_norm_g, attn_sink, w_out, router_group_w, router_group_b, router_expert_w, router_expert_b, expert_w_gate, expert_w_up, expert_w_down):
    batch, ctx_seq, d = x_prompt.shape
    dec_batch, lat_seq, _ = x_sample.shape
    depth = w_mod.shape[0]
    in_width = w_in.shape[2]
    conv_width, conv_ch = conv_dw_w.shape[1:]
    ret_heads = ret_decay.shape[2]
    ret_hd = state_retention.shape[-1]
    att_heads = attn_sink.shape[1]
    kv_heads, att_hd = cache_attn_k.shape[3:]
    past_len = cache_attn_k.shape[2]
    group = att_heads // kv_heads
    n_groups = router_group_w.shape[2]
    n_experts = router_expert_w.shape[2]
    d_expert = expert_w_gate.shape[3]
    nctx, nlat = batch * ctx_seq, dec_batch * lat_seq
    nt = nctx + nlat
    ret_w = ret_heads * ret_hd
    att_w = att_heads * att_hd
    kv_w = kv_heads * att_hd
    c_ca, c_cg = 0, conv_ch
    c_rq = 2 * conv_ch
    c_rk, c_rv, c_rg = c_rq + ret_w, c_rq + 2 * ret_w, c_rq + 3 * ret_w
    c_aq = c_rq + 4 * ret_w
    c_ak = c_aq + att_w
    c_av = c_ak + kv_w
    assert c_av + kv_w == in_width
    assert ctx_seq == CONV_TT == RET_TQ and lat_seq % ATT_TQ == 0 and INPROJ_TM == lat_seq
    assert conv_width // 2 <= CONV_HALO and n_experts + n_groups <= LANES
    assert d == 2 * PACK_SUB * LANES

    mod_rows = SUBLANES
    assert 1 + dec_batch <= mod_rows
    cv = jnp.concatenate([c_ctx[None, :], c, jnp.zeros((mod_rows - 1 - dec_batch, d), F32)], axis=0)
    mods = _modulation(cv, w_mod, b_mod).reshape(depth, mod_rows, 1, 6 * d)

    def mod_row(t0):
        return jnp.where(t0 < nctx, 0, (t0 - nctx) // lat_seq + 1)

    lg = jax.nn.log_sigmoid(ret_decay.astype(F32)).reshape(-1)
    sinks = attn_sink.astype(F32).reshape(-1)
    cos_t, sin_t = _rope_tables(lat_seq, att_hd)
    rw = jnp.concatenate([router_expert_w, router_group_w, jnp.zeros((depth, d, LANES - n_experts - n_groups), F32)], axis=2)
    rb = jnp.concatenate([router_expert_b, router_group_b, jnp.zeros((depth, LANES - n_experts - n_groups), F32)], axis=1)
    rb = rb.reshape(depth, 1, LANES)
    kc_all = cache_attn_k.reshape(dec_batch, depth, past_len, kv_w)
    vc_all = cache_attn_v.reshape(dec_batch, depth, past_len, kv_w)
    n_assign = nt * EXPERT_TOPK
    p_rows = n_assign + (n_experts + 1) * MOE_TM
    n_blk = p_rows // MOE_TM
    smem = pl.BlockSpec(memory_space=pltpu.SMEM)
    hbm = pl.BlockSpec(memory_space=pl.ANY)

    x = jnp.concatenate([x_prompt.reshape(nctx, d), x_sample.reshape(nlat, d)], axis=0)
    k_list, v_list, s_list = [], [], []
    for l in range(depth):
        n_j = in_width // INPROJ_TN
        proj = pl.pallas_call(
            _inproj_kernel,
            out_shape=jax.ShapeDtypeStruct((nt, in_width), BF16),
            grid=(nt // INPROJ_TM, n_j),
            in_specs=[
                pl.BlockSpec((INPROJ_TM, d), lambda i, j: (i, 0)),
                pl.BlockSpec((None, None, 1, d), lambda i, j, l=l: (l, mod_row(i * INPROJ_TM), 0, 1)),
                pl.BlockSpec((None, None, 1, d), lambda i, j, l=l: (l, mod_row(i * INPROJ_TM), 0, 0)),
                pl.BlockSpec((None, 1, d), lambda i, j, l=l: (l, 0, 0)),
                pl.BlockSpec((None, d, INPROJ_TN), lambda i, j, l=l: (l, 0, jnp.where(i == 0, j, n_j - 1))),
            ],
            out_specs=pl.BlockSpec((INPROJ_TM, INPROJ_TN), lambda i, j: (i, j)),
            scratch_shapes=[pltpu.VMEM((INPROJ_TM, d), BF16), pltpu.VMEM((n_j, d, INPROJ_TN), BF16)],
            compiler_params=_cparams(("arbitrary", "arbitrary")),
            name="inproj",
        )(x, mods, mods, norm1_g.reshape(depth, 1, d), w_in)

        hb = CONV_TT // CONV_HALO
        n_hb = nt // CONV_HALO
        conv_out = pl.pallas_call(
            functools.partial(_conv_kernel, nctx=nctx, ctx_seq=ctx_seq, lat_seq=lat_seq, width=conv_width),
            out_shape=jax.ShapeDtypeStruct((nt, conv_ch), BF16),
            grid=(nt // CONV_TT,),
            in_specs=[
                pl.BlockSpec((CONV_TT, conv_ch), lambda i: (i, c_ca // conv_ch)),
                pl.BlockSpec((CONV_TT, conv_ch), lambda i: (i, c_cg // conv_ch)),
                pl.BlockSpec((CONV_HALO, conv_ch), lambda i: (jnp.maximum(i * hb - 1, 0), c_ca // conv_ch)),
                pl.BlockSpec((CONV_HALO, conv_ch), lambda i: (jnp.maximum(i * hb - 1, 0), c_cg // conv_ch)),
                pl.BlockSpec((CONV_HALO, conv_ch), lambda i: (jnp.minimum((i + 1) * hb, n_hb - 1), c_ca // conv_ch)),
                pl.BlockSpec((CONV_HALO, conv_ch), lambda i: (jnp.minimum((i + 1) * hb, n_hb - 1), c_cg // conv_ch)),
                pl.BlockSpec((None, conv_width, conv_ch), lambda i, l=l: (l, 0, 0)),
                pl.BlockSpec((None, 1, conv_ch), lambda i, l=l: (l, 0, 0)),
                pl.BlockSpec((None, 1, conv_ch), lambda i, l=l: (l, 0, 0)),
                pl.BlockSpec((None, 1, conv_ch), lambda i, l=l: (l, 0, 0)),
            ],
            out_specs=pl.BlockSpec((CONV_TT, conv_ch), lambda i: (i, 0)),
            scratch_shapes=[pltpu.VMEM((SUBLANES, CONV_TT + 2 * CONV_HALO, conv_ch), F32)],
            compiler_params=_cparams(("arbitrary",)),
            name="conv",
        )(proj, proj, proj, proj, proj, proj, conv_dw_w, conv_dw_b.reshape(depth, 1, conv_ch),
          conv_ln_g.reshape(depth, 1, conv_ch), conv_ln_b.reshape(depth, 1, conv_ch))

        def ret_call(nseq, seq, row0, has_s0, emit_state):
            nq = seq // RET_TQ
            in_specs = [
                smem,
                pl.BlockSpec((RET_TQ, ret_w), lambda b, qi: (row0 // RET_TQ + b * nq + qi, c_rq // ret_w)),
                pl.BlockSpec((seq, ret_w), lambda b, qi: (row0 // seq + b, c_rk // ret_w)),
                pl.BlockSpec((seq, ret_w), lambda b, qi: (row0 // seq + b, c_rv // ret_w)),
                pl.BlockSpec((RET_TQ, ret_w), lambda b, qi: (row0 // RET_TQ + b * nq + qi, c_rg // ret_w)),
            ]
            args = [lg, proj, proj, proj, proj]
            if has_s0:
                in_specs.append(pl.BlockSpec((None, None, 2, ret_heads, ret_hd, ret_hd),
                                             lambda b, qi, l=l: (b, l, 0, 0, 0, 0)))
                args.append(state_retention)
            out_shape = [jax.ShapeDtypeStruct((nseq * seq, ret_w), BF16)]
            out_specs = [pl.BlockSpec((RET_TQ, ret_w), lambda b, qi: (b * nq + qi, 0))]
            if emit_state:
                out_shape.append(jax.ShapeDtypeStruct((nseq, 2, ret_heads, ret_hd, ret_hd), F32))
                out_specs.append(pl.BlockSpec((None, 2, ret_heads, ret_hd, ret_hd), lambda b, qi: (b, 0, 0, 0, 0)))
            return pl.pallas_call(
                functools.partial(_ret_kernel, layer=l, heads=ret_heads, seq=seq, has_s0=has_s0, emit_state=emit_state),
                out_shape=out_shape,
                grid=(nseq, nq),
                in_specs=in_specs,
                out_specs=out_specs,
                compiler_params=_cparams(("arbitrary", "arbitrary")),
                name="retention",
            )(*args)

        ret_ctx, s_l = ret_call(batch, ctx_seq, 0, False, True)
        (ret_lat,) = ret_call(dec_batch, lat_seq, nctx, True, False)
        s_list.append(s_l)

        gq = q_norm_g.reshape(depth, 1, att_hd)
        gk = k_norm_g.reshape(depth, 1, att_hd)
        att_ctx, k_l, v_l = pl.pallas_call(
            functools.partial(_attn_ctx_kernel, layer=l, group=group, scale=att_hd ** -0.5, nkv=kv_heads),
            out_shape=[jax.ShapeDtypeStruct((nctx, att_w), BF16),
                       jax.ShapeDtypeStruct((batch, ctx_seq, kv_w), F32),
                       jax.ShapeDtypeStruct((batch, ctx_seq, kv_w), F32)],
            grid=(batch,),
            in_specs=[
                smem,
                pl.BlockSpec((ctx_seq, att_w), lambda b: (b, c_aq // att_w)),
                pl.BlockSpec((ctx_seq, kv_w), lambda b: (b, c_ak // kv_w)),
                pl.BlockSpec((ctx_seq, kv_w), lambda b: (b, c_av // kv_w)),
                pl.BlockSpec((None, 1, att_hd), lambda b, l=l: (l, 0, 0)),
                pl.BlockSpec((None, 1, att_hd), lambda b, l=l: (l, 0, 0)),
            ],
            out_specs=[
                pl.BlockSpec((ctx_seq, att_w), lambda b: (b, 0)),
                pl.BlockSpec((None, ctx_seq, kv_w), lambda b: (b, 0, 0)),
                pl.BlockSpec((None, ctx_seq, kv_w), lambda b: (b, 0, 0)),
            ],
            compiler_params=_cparams(("arbitrary",)),
            name="attn_ctx",
        )(sinks, proj, proj, proj, gq, gk)
        k_list.append(k_l)
        v_list.append(v_l)

        nq = lat_seq // ATT_TQ
        att_lat = pl.pallas_call(
            functools.partial(_attn_lat_kernel, layer=l, group=group, scale=att_hd ** -0.5, seq=lat_seq, nkv=kv_heads),
            out_shape=jax.ShapeDtypeStruct((nlat, att_w), BF16),
            grid=(dec_batch, nq),
            in_specs=[
                smem,
                pl.BlockSpec((ATT_TQ, att_w), lambda b, qi: (nctx // ATT_TQ + b * nq + qi, c_aq // att_w)),
                pl.BlockSpec((lat_seq, kv_w), lambda b, qi: (nctx // lat_seq + b, c_ak // kv_w)),
                pl.BlockSpec((lat_seq, kv_w), lambda b, qi: (nctx // lat_seq + b, c_av // kv_w)),
                pl.BlockSpec((None, None, past_len, kv_w), lambda b, qi, l=l: (b, l, 0, 0)),
                pl.BlockSpec((None, None, past_len, kv_w), lambda b, qi, l=l: (b, l, 0, 0)),
                pl.BlockSpec((None, 1, att_hd), lambda b, qi, l=l: (l, 0, 0)),
                pl.BlockSpec((None, 1, att_hd), lambda b, qi, l=l: (l, 0, 0)),
                pl.BlockSpec((lat_seq, att_hd), lambda b, qi: (0, 0)),
                pl.BlockSpec((lat_seq, att_hd), lambda b, qi: (0, 0)),
            ],
            out_specs=pl.BlockSpec((ATT_TQ, att_w), lambda b, qi: (b * nq + qi, 0)),
            scratch_shapes=[pltpu.VMEM((lat_seq, kv_w), BF16)],
            compiler_params=_cparams(("arbitrary", "arbitrary")),
            name="attn_lat",
        )(sinks, proj, proj, proj, kc_all, vc_all, gq, gk, cos_t, sin_t)

        ctx_tiles = nctx // OUT_TM
        lat_tiles = nlat // OUT_TM

        def ctx_rows(i):
            return jnp.minimum(i, ctx_tiles - 1)

        def lat_rows(i):
            return jnp.maximum(i - ctx_tiles, 0)

        def tile_mod(chunk):
            return pl.BlockSpec((None, None, 1, d), lambda s, l=l: (l, mod_row(s * OUT_TM), 0, chunk))

        x1, h2, route, cnt = pl.pallas_call(
            functools.partial(_outproj_kernel, n_groups=n_groups, n_experts=n_experts, ctx_tiles=ctx_tiles),
            out_shape=[jax.ShapeDtypeStruct((nt, d), F32), jax.ShapeDtypeStruct((nt * PACK_SUB, LANES), jnp.uint32),
                       jax.ShapeDtypeStruct((nt, LANES), F32), jax.ShapeDtypeStruct((1, LANES), F32)],
            grid=(ctx_tiles + lat_tiles,),
            in_specs=[
                pl.BlockSpec((OUT_TM, conv_ch), lambda s: (s, 0)),
                pl.BlockSpec((OUT_TM, ret_w), lambda s: (ctx_rows(s), 0)),
                pl.BlockSpec((OUT_TM, ret_w), lambda s: (lat_rows(s), 0)),
                pl.BlockSpec((OUT_TM, att_w), lambda s: (ctx_rows(s), 0)),
                pl.BlockSpec((OUT_TM, att_w), lambda s: (lat_rows(s), 0)),
                pl.BlockSpec((OUT_TM, d), lambda s: (s, 0)),
                tile_mod(2),
                tile_mod(4),
                tile_mod(3),
                pl.BlockSpec((None, 1, d), lambda s, l=l: (l, 0, 0)),
                pl.BlockSpec((None, d, d), lambda s, l=l: (l, 0, 0), pipeline_mode=pl.Buffered(1)),
                pl.BlockSpec((None, d, LANES), lambda s, l=l: (l, 0, 0), pipeline_mode=pl.Buffered(1)),
                pl.BlockSpec((None, 1, LANES), lambda s, l=l: (l, 0, 0)),
            ],
            out_specs=[
                pl.BlockSpec((OUT_TM, d), lambda s: (s, 0)),
                pl.BlockSpec((OUT_TM * PACK_SUB, LANES), lambda s: (s, 0)),
                pl.BlockSpec((OUT_TM, LANES), lambda s: (s, 0)),
                pl.BlockSpec((1, LANES), lambda s: (0, 0)),
            ],
            scratch_shapes=[pltpu.VMEM((d, d), BF16), pltpu.VMEM((d, LANES), BF16), pltpu.VMEM((1, LANES), F32)],
            compiler_params=_cparams(("arbitrary",)),
            name="outproj_router",
        )(conv_out, ret_ctx, ret_lat, att_ctx, att_lat, x, mods, mods, mods, norm2_g.reshape(depth, 1, d), w_out, rw, rb)

        eid = route[:, 0:EXPERT_TOPK].astype(I32)
        rank = route[:, 4:4 + EXPERT_TOPK].astype(I32)
        counts = cnt[0, :n_experts].astype(I32)
        padded = ((counts + MOE_TM - 1) // MOE_TM) * MOE_TM
        pends = jnp.cumsum(padded)
        pstarts = pends - padded
        ek = jnp.arange(n_experts, dtype=I32)
        dest = rank + jnp.sum(jnp.where(eid[:, :, None] == ek, pstarts, 0), axis=-1)
        dest0, dest1 = dest[:, 0], dest[:, 1]
        blk_start = jnp.arange(n_blk, dtype=I32) * MOE_TM
        blk_e = jnp.minimum(jnp.sum((pends[None, :] <= blk_start[:, None]).astype(I32), axis=1), n_experts - 1)
        later = (ek[None, :] > ek[:, None]) & (counts[None, :] > 0)
        nxt_e = jnp.min(jnp.where(later, ek[None, :], n_experts), axis=1)
        nxt_e = jnp.where(nxt_e == n_experts, -1, nxt_e).astype(I32)
        nused = (pends[-1:] // MOE_TM).astype(I32)

        ys = pl.pallas_call(
            functools.partial(_expert_kernel, layer=l),
            out_shape=jax.ShapeDtypeStruct((p_rows * PACK_SUB, LANES), jnp.uint32),
            grid_spec=pltpu.PrefetchScalarGridSpec(
                num_scalar_prefetch=5,
                grid=(n_blk,),
                in_specs=[hbm, hbm, hbm, hbm],
                out_specs=pl.BlockSpec((MOE_TM * PACK_SUB, LANES), lambda b, *_: (b, 0)),
                scratch_shapes=[pltpu.SMEM((p_rows,), I32),
                                pltpu.VMEM((2, MOE_TM * PACK_SUB, LANES), jnp.uint32), pltpu.VMEM((MOE_TM, d), BF16),
                                pltpu.VMEM((MOE_TM, d_expert), BF16),
                                pltpu.VMEM((d, d_expert), F32), pltpu.VMEM((d, d_expert), F32),
                                pltpu.VMEM((d_expert, d), F32),
                                pltpu.VMEM((d, d_expert), BF16), pltpu.VMEM((d, d_expert), BF16),
                                pltpu.VMEM((d_expert, d), BF16),
                                pltpu.SemaphoreType.DMA((2,)), pltpu.SemaphoreType.DMA((3,))],
            ),
            compiler_params=_cparams(("arbitrary",)),
            name="experts",
        )(blk_e, dest0, dest1, nxt_e, nused, h2, expert_w_gate, expert_w_up, expert_w_down)

        x = pl.pallas_call(
            _combine_kernel,
            out_shape=jax.ShapeDtypeStruct((nt, d), F32),
            grid_spec=pltpu.PrefetchScalarGridSpec(
                num_scalar_prefetch=2,
                grid=(nt // CMB_TT,),
                in_specs=[
                    hbm,
                    pl.BlockSpec((CMB_TT, d), lambda i, d0, d1: (i, 0)),
                    pl.BlockSpec((CMB_TT, LANES), lambda i, d0, d1: (i, 0)),
                    pl.BlockSpec((None, None, 1, d), lambda i, d0, d1, l=l: (l, mod_row(i * CMB_TT), 0, 5)),
                ],
                out_specs=pl.BlockSpec((CMB_TT, d), lambda i, d0, d1: (i, 0)),
                scratch_shapes=[pltpu.VMEM((2, EXPERT_TOPK, CMB_TT * PACK_SUB, LANES), jnp.uint32),
                                pltpu.SemaphoreType.DMA((2,))],
            ),
            compiler_params=_cparams(("arbitrary",)),
            name="combine",
        )(dest0, dest1, ys, x1, route, mods)

    y_prompt = x[:nctx].reshape(batch, ctx_seq, d)
    y_sample = x[nctx:].reshape(dec_batch, lat_seq, d)
    new_k = jnp.stack(k_list, axis=1).reshape(batch, depth, ctx_seq, kv_heads, att_hd)
    new_v = jnp.stack(v_list, axis=1).reshape(batch, depth, ctx_seq, kv_heads, att_hd)
    new_s = jnp.stack(s_list, axis=1)
    return (y_prompt, y_sample, new_k, new_v, new_s)
```

```python
import functools

import jax
import jax.numpy as jnp
from jax import lax
from jax.experimental import pallas as pl
from jax.experimental.pallas import tpu as pltpu

F32, BF16, I32 = jnp.float32, jnp.bfloat16, jnp.int32

GRID_W = 64
ATT_WINDOW = 128
ROPE_BASE = 10000.0
EXPERT_TOPK = 2
EPS = 1e-6
NEG_INF = -1e30

LANES = 128
SUBLANES = 8
VMEM_LIMIT_BYTES = 56 * 1024 * 1024

MOD_TN = 1024
INPROJ_TM = 1024
INPROJ_TN = 512
CONV_TT = 256
CONV_HALO = 16
CONV_RC = 32
RET_TQ = 256
ATT_TQ = 256
OUT_TM = 256
MOE_TM = 128
CMB_TT = 256
GATHER_UNROLL = 8
GATHER_BURSTS = 4
SCALAR_UNROLL = 16
WEIGHT_DMA_PRIORITY = 1


def _cparams(sem):
    return pltpu.CompilerParams(dimension_semantics=sem, vmem_limit_bytes=VMEM_LIMIT_BYTES)


def _rms(x, g):
    return x * lax.rsqrt(jnp.mean(x * x, axis=-1, keepdims=True) + EPS) * g


def _silu(x):
    return x * jax.nn.sigmoid(x)


def _dot(a, b):
    return jnp.dot(a, b, preferred_element_type=F32)


def _dot_nt(a, b):
    return lax.dot_general(a, b, (((1,), (1,)), ((), ())), preferred_element_type=F32)


def _dot_tn(a, b):
    return lax.dot_general(a, b, (((0,), (0,)), ((), ())), preferred_element_type=F32)


def _mod_kernel(cv_ref, w_ref, b_ref, o_ref):
    s = _silu(cv_ref[...]).astype(BF16)
    o_ref[...] = _dot(s, w_ref[...].astype(BF16)) + b_ref[...]


def _modulation(cv, w_mod, b_mod):
    depth, d, n6 = w_mod.shape
    rows = cv.shape[0]
    return pl.pallas_call(
        _mod_kernel,
        out_shape=jax.ShapeDtypeStruct((depth, rows, n6), F32),
        grid=(depth, n6 // MOD_TN),
        in_specs=[
            pl.BlockSpec((rows, d), lambda l, j: (0, 0)),
            pl.BlockSpec((None, d, MOD_TN), lambda l, j: (l, 0, j)),
            pl.BlockSpec((None, 1, MOD_TN), lambda l, j: (l, 0, j)),
        ],
        out_specs=pl.BlockSpec((None, rows, MOD_TN), lambda l, j: (l, 0, j)),
        compiler_params=_cparams(("arbitrary", "arbitrary")),
        name="modulation",
    )(cv, w_mod, b_mod.reshape(depth, 1, n6))


def _inproj_kernel(x_ref, sc_ref, sh_ref, g_ref, w_ref, o_ref, h_ref, wb_ref):
    i = pl.program_id(0)
    j = pl.program_id(1)

    @pl.when(i == 0)
    def _():
        wb_ref[j] = w_ref[...].astype(BF16)

    @pl.when(j == 0)
    def _():
        h = _rms(x_ref[...], g_ref[...]) * (1.0 + sc_ref[...]) + sh_ref[...]
        h_ref[...] = h.astype(BF16)

    o_ref[...] = _dot(h_ref[...], wb_ref[j]).astype(BF16)


def _conv_kernel(a_ref, g_ref, ap_ref, gp_ref, an_ref, gn_ref, w_ref, b_ref, lng_ref, lnb_ref, o_ref,
                 hs_ref, *, nctx, ctx_seq, lat_seq, width):
    t0 = pl.program_id(0) * CONV_TT
    is_ctx = t0 < nctx
    seq_len = jnp.where(is_ctx, ctx_seq, lat_seq)
    pos = lax.rem(jnp.where(is_ctx, t0, t0 - nctx), seq_len)
    first = pos == 0
    last = pos + CONV_TT == seq_len
    rows = CONV_TT + 2 * CONV_HALO
    span = rows - SUBLANES
    def glu(a, g):
        return a[...].astype(F32) * jax.nn.sigmoid(g[...].astype(F32))

    hs_ref[0, 0:CONV_HALO, :] = jnp.where(first, 0.0, glu(ap_ref, gp_ref))
    hs_ref[0, CONV_HALO:CONV_HALO + CONV_TT, :] = glu(a_ref, g_ref)
    hs_ref[0, CONV_HALO + CONV_TT:rows, :] = jnp.where(last, 0.0, glu(an_ref, gn_ref))
    for s in range(1, SUBLANES):
        hs_ref[s, 0:span, :] = hs_ref[0, s:s + span, :]
    base = CONV_HALO - width // 2
    for r0 in range(0, CONV_TT, CONV_RC):
        acc = jnp.zeros((CONV_RC, o_ref.shape[1]), F32)
        for k in range(width):
            idx = k + base
            s, q = idx % SUBLANES, idx // SUBLANES
            acc = acc + w_ref[k:k + 1, :] * hs_ref[s, r0 + SUBLANES * q:r0 + SUBLANES * q + CONV_RC, :]
        hf = acc + b_ref[...]
        mu = jnp.mean(hf, axis=-1, keepdims=True)
        var = jnp.mean(jnp.square(hf - mu), axis=-1, keepdims=True)
        y = (hf - mu) * lax.rsqrt(var + EPS) * lng_ref[...] + lnb_ref[...]
        o_ref[r0:r0 + CONV_RC, :] = _silu(y).astype(BF16)


def _ret_kernel(lg_ref, q_ref, k_ref, v_ref, gate_ref, *rest, layer, heads, seq, has_s0, emit_state):
    rest = list(rest)
    s0_ref = rest.pop(0) if has_s0 else None
    o_ref = rest.pop(0)
    st_ref = rest.pop(0) if emit_state else None
    qi = pl.program_id(1)
    dh = q_ref.shape[1] // heads
    n_idx = qi * RET_TQ + lax.broadcasted_iota(I32, (RET_TQ, seq), 0)
    m_idx = lax.broadcasted_iota(I32, (RET_TQ, seq), 1)
    rel = (n_idx - m_idx).astype(F32)
    npos = (qi * RET_TQ + lax.broadcasted_iota(I32, (RET_TQ, 1), 0)).astype(F32)
    mpos = lax.broadcasted_iota(I32, (seq, 1), 0).astype(F32)
    for h in range(heads):
        cols = slice(h * dh, (h + 1) * dh)
        lgf = lg_ref[layer * 2 * heads + h]
        lgb = lg_ref[layer * 2 * heads + heads + h]
        q = q_ref[:, cols].astype(F32)
        k = k_ref[:, cols].astype(F32) * (dh ** -0.5)
        vb = v_ref[:, cols]
        a = _dot_nt(q_ref[:, cols], k.astype(BF16))
        decay = jnp.exp(jnp.where(rel >= 0.0, lgf, -lgb) * rel)
        o = _dot((a * decay).astype(BF16), vb)
        if has_s0:
            cf = jnp.exp(lgf * (npos + 1.0))
            cb = jnp.exp(lgb * (seq - npos))
            o = (o + _dot((q * cf).astype(BF16), s0_ref[0, h].astype(BF16))
                 + _dot((q * cb).astype(BF16), s0_ref[1, h].astype(BF16)))
        y = o * lax.rsqrt(jnp.mean(o * o, axis=-1, keepdims=True) + EPS)
        o_ref[:, cols] = (_silu(gate_ref[:, cols].astype(F32)) * y).astype(BF16)
        if emit_state:
            st_ref[0, h] = _dot_tn((k * jnp.exp(lgf * (seq - 1.0 - mpos))).astype(BF16), vb)
            st_ref[1, h] = _dot_tn((k * jnp.exp(lgb * mpos)).astype(BF16), vb)


def _attn_ctx_kernel(sink_ref, q_ref, k_ref, v_ref, gq_ref, gk_ref, o_ref, ko_ref, vo_ref, *, layer, group, scale, nkv):
    dh = k_ref.shape[1] // nkv
    for kv in range(nkv):
        kcols = slice(kv * dh, (kv + 1) * dh)
        kn = _rms(k_ref[:, kcols].astype(F32), gk_ref[...])
        vb = v_ref[:, kcols]
        ko_ref[:, kcols] = kn
        vo_ref[:, kcols] = vb.astype(F32)
        knb = kn.astype(BF16)
        for g in range(group):
            head = kv * group + g
            cols = slice(head * dh, (head + 1) * dh)
            sink = sink_ref[layer * nkv * group + head]
            qn = _rms(q_ref[:, cols].astype(F32), gq_ref[...])
            s = _dot_nt(qn.astype(BF16), knb) * scale
            m = jnp.maximum(jnp.max(s, axis=-1, keepdims=True), sink)
            p = jnp.exp(s - m)
            den = jnp.sum(p, axis=-1, keepdims=True) + jnp.exp(sink - m)
            o_ref[:, cols] = (_dot(p.astype(BF16), vb) / den).astype(BF16)


def _rope(x, cos, sin_signed):
    lane = lax.broadcasted_iota(I32, x.shape, 1)
    half = x.shape[1] // 4
    swapped = jnp.where(lax.rem(lane, 2 * half) < half, pltpu.roll(x, x.shape[1] - half, 1), pltpu.roll(x, half, 1))
    return x * cos + swapped * sin_signed


def _attn_lat_kernel(sink_ref, q_ref, k_ref, v_ref, kc_ref, vc_ref, gq_ref, gk_ref, cos_ref, sin_ref,
                     o_ref, kr_ref, *, layer, group, scale, seq, nkv):
    qi = pl.program_id(1)
    dh = k_ref.shape[1] // nkv
    win = ATT_TQ + 2 * ATT_WINDOW

    @pl.when(qi == 0)
    def _():
        for kv in range(nkv):
            kcols = slice(kv * dh, (kv + 1) * dh)
            kn = _rms(k_ref[:, kcols].astype(F32), gk_ref[...])
            kr_ref[:, kcols] = _rope(kn, cos_ref[...], sin_ref[...]).astype(BF16)

    q0 = pl.multiple_of(qi * ATT_TQ, ATT_TQ)
    start = pl.multiple_of(jnp.clip(qi * ATT_TQ - ATT_WINDOW, 0, seq - win), ATT_WINDOW)
    qpos = q0 + lax.broadcasted_iota(I32, (ATT_TQ, win), 0)
    kpos = start + lax.broadcasted_iota(I32, (ATT_TQ, win), 1)
    valid = jnp.abs(qpos - kpos) <= ATT_WINDOW
    cs = cos_ref[pl.ds(q0, ATT_TQ), :]
    sn = sin_ref[pl.ds(q0, ATT_TQ), :]
    for kv in range(nkv):
        kcols = slice(kv * dh, (kv + 1) * dh)
        kw = kr_ref[pl.ds(start, win), kcols]
        vw = v_ref[pl.ds(start, win), kcols]
        kcb = kc_ref[:, kcols].astype(BF16)
        vcb = vc_ref[:, kcols].astype(BF16)
        for g in range(group):
            head = kv * group + g
            cols = slice(head * dh, (head + 1) * dh)
            sink = sink_ref[layer * nkv * group + head]
            qr = _rope(_rms(q_ref[:, cols].astype(F32), gq_ref[...]), cs, sn).astype(BF16)
            s_loc = jnp.where(valid, _dot_nt(qr, kw) * scale, NEG_INF)
            s_ctx = _dot_nt(qr, kcb) * scale
            m = jnp.maximum(jnp.maximum(jnp.max(s_loc, axis=-1, keepdims=True),
                                        jnp.max(s_ctx, axis=-1, keepdims=True)), sink)
            p_loc = jnp.exp(s_loc - m)
            p_ctx = jnp.exp(s_ctx - m)
            den = jnp.sum(p_loc, axis=-1, keepdims=True) + jnp.sum(p_ctx, axis=-1, keepdims=True) + jnp.exp(sink - m)
            o = _dot(p_loc.astype(BF16), vw) + _dot(p_ctx.astype(BF16), vcb)
            o_ref[:, cols] = (o / den).astype(BF16)


def _outproj_kernel(conv_ref, retc_ref, retl_ref, attc_ref, attl_ref, x_ref, g1_ref, sc2_ref, sh2_ref, n2_ref, wout_ref,
                    rw_ref, rb_ref, x1_ref, h2_ref, route_ref, cnt_ref, wb_ref, rwb_ref, run_ref,
                    *, n_groups, n_experts, ctx_tiles):
    s = pl.program_id(0)
    c_conv = conv_ref.shape[1]
    c_ret = retc_ref.shape[1]
    is_ctx = s < ctx_tiles
    ret = jnp.where(is_ctx, retc_ref[...], retl_ref[...])
    att = jnp.where(is_ctx, attc_ref[...], attl_ref[...])

    @pl.when(s == 0)
    def _():
        wb_ref[...] = wout_ref[...].astype(BF16)
        rwb_ref[...] = rw_ref[...].astype(BF16)
        run_ref[...] = jnp.zeros_like(run_ref)

    def step():
        mix = (_dot(conv_ref[...], wb_ref[0:c_conv, :]) + _dot(ret, wb_ref[c_conv:c_conv + c_ret, :])
               + _dot(att, wb_ref[c_conv + c_ret:, :]))
        x1 = x_ref[...] + g1_ref[...] * mix
        x1_ref[...] = x1
        h2 = _rms(x1, n2_ref[...]) * (1.0 + sc2_ref[...]) + sh2_ref[...]
        _pack_rows(h2, h2_ref)
        lg = _dot(h2.astype(BF16), rwb_ref[...]) + rb_ref[...]
        tm = lg.shape[0]
        lane = lax.broadcasted_iota(I32, lg.shape, 1)
        lanef = lane.astype(F32)
        big = float(LANES)
        gmask = (lane >= n_experts) & (lane < n_experts + n_groups)
        gl = jnp.where(gmask, lg, NEG_INF)
        gmax = jnp.max(gl, axis=-1, keepdims=True)
        gidx = jnp.min(jnp.where(gl == gmax, lanef, big), axis=-1, keepdims=True) - float(n_experts)
        gp = 1.0 / jnp.sum(jnp.where(gmask, jnp.exp(gl - gmax), 0.0), axis=-1, keepdims=True)
        per_group = n_experts // n_groups
        emask = (lanef >= gidx * per_group) & (lanef < (gidx + 1.0) * per_group)
        el = jnp.where(emask, lg, NEG_INF)
        v1 = jnp.max(el, axis=-1, keepdims=True)
        i1 = jnp.min(jnp.where(el == v1, lanef, big), axis=-1, keepdims=True)
        el2 = jnp.where(lanef == i1, NEG_INF, el)
        v2 = jnp.max(el2, axis=-1, keepdims=True)
        i2 = jnp.min(jnp.where(el2 == v2, lanef, big), axis=-1, keepdims=True)
        t = jnp.exp(v2 - v1)
        w1 = gp / (1.0 + t)
        w2 = gp * t / (1.0 + t)
        sel1 = lanef == i1
        sel2 = lanef == i2
        oh = jnp.where(sel1 | sel2, 1.0, 0.0)
        r_i = lax.broadcasted_iota(I32, (tm, tm), 0)
        c_i = lax.broadcasted_iota(I32, (tm, tm), 1)
        lower = jnp.where(c_i < r_i, 1.0, 0.0).astype(BF16)
        before = _dot(lower, oh.astype(BF16)) + run_ref[...]
        rank1 = jnp.sum(jnp.where(sel1, before, 0.0), axis=-1, keepdims=True)
        rank2 = jnp.sum(jnp.where(sel2, before, 0.0), axis=-1, keepdims=True)
        run_ref[...] = run_ref[...] + jnp.sum(oh, axis=0, keepdims=True)
        cnt_ref[...] = run_ref[...]
        route = jnp.where(lane == 0, i1, jnp.where(lane == 1, i2, jnp.where(lane == 2, w1, jnp.where(
            lane == 3, w2, jnp.where(lane == 4, rank1, jnp.where(lane == 5, rank2, 0.0))))))
        route_ref[...] = route

    step()


PACK_SUB = SUBLANES


def _pack_cols(lo_v, hi_v, o_ref, c0):
    tm = lo_v.shape[0]
    lo_bits = pltpu.bitcast(lo_v.astype(BF16).astype(F32), jnp.uint32)
    hi_bits = pltpu.bitcast(hi_v.astype(BF16).astype(F32), jnp.uint32)
    for j in range(lo_v.shape[1] // LANES):
        lo = lax.shift_right_logical(lo_bits[:, j * LANES:(j + 1) * LANES], jnp.uint32(16))
        hi = hi_bits[:, j * LANES:(j + 1) * LANES] & jnp.uint32(0xFFFF0000)
        o_ref[pl.ds(c0 + j, tm, stride=PACK_SUB), :] = lo | hi


def _pack_rows(v, o_ref):
    half = PACK_SUB * LANES
    _pack_cols(v[:, :half], v[:, half:], o_ref, 0)


def _unpack_chunk(src_ref, c, rows):
    w = src_ref[pl.ds(c, rows, stride=PACK_SUB), :]
    lo = pltpu.bitcast(lax.shift_left(w, jnp.uint32(16)), F32)
    hi = pltpu.bitcast(w & jnp.uint32(0xFFFF0000), F32)
    return lo, hi


def _expert_kernel(blk_e_ref, d0_ref, d1_ref, nxt_e_ref, nused_ref, h2_hbm, wg_hbm, wu_hbm, wd_hbm, ys_ref,
                   row_tok_ref, xbuf, xb, hb, wg_stg, wu_stg, wd_stg, wgb, wub, wdb, sem, wsem, *, layer):
    b = pl.program_id(0)
    nused = nused_ref[0]
    slot = lax.rem(b, 2)

    @pl.when(b == 0)
    def _():
        def clear(r, carry):
            row_tok_ref[r] = 0
            return carry

        def fill(t, carry):
            row_tok_ref[d0_ref[t]] = t
            row_tok_ref[d1_ref[t]] = t
            return carry

        lax.fori_loop(0, row_tok_ref.shape[0], clear, 0, unroll=SCALAR_UNROLL)
        lax.fori_loop(0, d0_ref.shape[0], fill, 0, unroll=SCALAR_UNROLL)

    def gather_start(blk, sl, first=0, count=MOE_TM):
        def body(r, carry):
            src = pl.multiple_of(row_tok_ref[blk * MOE_TM + r] * PACK_SUB, PACK_SUB)
            dst = pl.multiple_of(r * PACK_SUB, PACK_SUB)
            pltpu.make_async_copy(h2_hbm.at[pl.ds(src, PACK_SUB), :], xbuf.at[sl, pl.ds(dst, PACK_SUB), :],
                                  sem.at[sl]).start()
            return carry

        lax.fori_loop(first, first + count, body, 0, unroll=GATHER_UNROLL)

    def gather_wait(sl):
        pltpu.make_async_copy(h2_hbm.at[pl.ds(0, MOE_TM * PACK_SUB), :], xbuf.at[sl], sem.at[sl]).wait()

    def weight_copies(e):
        return (pltpu.make_async_copy(wg_hbm.at[layer, e], wg_stg, wsem.at[0]),
                pltpu.make_async_copy(wu_hbm.at[layer, e], wu_stg, wsem.at[1]),
                pltpu.make_async_copy(wd_hbm.at[layer, e], wd_stg, wsem.at[2]))

    @pl.when((b == 0) & (nused > 0))
    def _():
        gather_start(0, 0)
        for cp in weight_copies(blk_e_ref[0]):
            cp.start(priority=WEIGHT_DMA_PRIORITY)

    @pl.when(b == nused)
    def _():
        gather_wait(slot)

    @pl.when(b < nused)
    def _():
        e = blk_e_ref[b]
        prev = blk_e_ref[jnp.maximum(b - 1, 0)]

        @pl.when((b == 0) | (e != prev))
        def _():
            for cp in weight_copies(e):
                cp.wait()
            wgb[...] = wg_stg[...].astype(BF16)
            wub[...] = wu_stg[...].astype(BF16)
            wdb[...] = wd_stg[...].astype(BF16)
            nxt = nxt_e_ref[e]

            @pl.when(nxt >= 0)
            def _():
                for cp in weight_copies(nxt):
                    cp.start(priority=WEIGHT_DMA_PRIORITY)

        gather_wait(slot)
        burst = MOE_TM // GATHER_BURSTS
        gather_start(b + 1, 1 - slot, 0, burst)
        half = PACK_SUB * LANES
        for c in range(PACK_SUB):
            lo, hi = _unpack_chunk(xbuf.at[slot], c, MOE_TM)
            xb[:, c * LANES:(c + 1) * LANES] = lo.astype(BF16)
            xb[:, half + c * LANES:half + (c + 1) * LANES] = hi.astype(BF16)
        gather_start(b + 1, 1 - slot, burst, burst)
        x = xb[...]
        hb[...] = (_silu(_dot(x, wgb[...])) * _dot(x, wub[...])).astype(BF16)
        gather_start(b + 1, 1 - slot, 2 * burst, burst)
        _pack_rows(_dot(hb[...], wdb[...]), ys_ref)
        gather_start(b + 1, 1 - slot, 3 * burst, MOE_TM - 3 * burst)

    @pl.when(b >= nused)
    def _():
        ys_ref[...] = jnp.zeros_like(ys_ref)


def _combine_kernel(d0_ref, d1_ref, ys_hbm, x1_ref, route_ref, g2_ref, o_ref, ybuf, sem):
    i = pl.program_id(0)
    n = pl.num_programs(0)
    slot = lax.rem(i, 2)

    def gather_start(tile, sl):
        def body(r, carry):
            t = tile * CMB_TT + r
            dst = pl.multiple_of(r * PACK_SUB, PACK_SUB)
            for j, d_ref in enumerate((d0_ref, d1_ref)):
                src = pl.multiple_of(d_ref[t] * PACK_SUB, PACK_SUB)
                pltpu.make_async_copy(ys_hbm.at[pl.ds(src, PACK_SUB), :], ybuf.at[sl, j, pl.ds(dst, PACK_SUB), :],
                                      sem.at[sl]).start(priority=j)
            return carry

        lax.fori_loop(0, CMB_TT, body, 0, unroll=GATHER_UNROLL)

    def gather_wait(sl):
        for j in range(EXPERT_TOPK):
            pltpu.make_async_copy(ys_hbm.at[pl.ds(0, CMB_TT * PACK_SUB), :], ybuf.at[sl, j], sem.at[sl]).wait()

    @pl.when(i == 0)
    def _():
        gather_start(0, 0)

    @pl.when(i + 1 < n)
    def _():
        gather_start(i + 1, 1 - slot)

    gather_wait(slot)
    w0 = route_ref[:, 2:3]
    w1 = route_ref[:, 3:4]
    half = PACK_SUB * LANES
    for c in range(PACK_SUB):
        lo0, hi0 = _unpack_chunk(ybuf.at[slot, 0], c, CMB_TT)
        lo1, hi1 = _unpack_chunk(ybuf.at[slot, 1], c, CMB_TT)
        for off, y0, y1 in ((c * LANES, lo0, lo1), (half + c * LANES, hi0, hi1)):
            cols = slice(off, off + LANES)
            o_ref[:, cols] = x1_ref[:, cols] + g2_ref[:, cols] * (y0 * w0 + y1 * w1)


def _rope_tables(seq, dh):
    nf = dh // 4
    t = jnp.arange(seq)
    inv = ROPE_BASE ** (-jnp.arange(nf, dtype=F32) / nf)
    ang_r = (t // GRID_W).astype(F32)[:, None] * inv[None, :]
    ang_c = (t % GRID_W).astype(F32)[:, None] * inv[None, :]
    cos = jnp.concatenate([jnp.cos(ang_r)] * 2 + [jnp.cos(ang_c)] * 2, axis=-1)
    sin = jnp.concatenate([-jnp.sin(ang_r), jnp.sin(ang_r), -jnp.sin(ang_c), jnp.sin(ang_c)], axis=-1)
    return cos, sin


def kernel(x_prompt, x_sample, cache_attn_k, cache_attn_v, state_retention, c, c_ctx, w_mod, b_mod, norm1_g, norm2_g, w_in, conv_dw_w, conv_dw_b, conv_ln_g, conv_ln_b, ret_decay, q_norm_g, k_norm_g, attn_sink, w_out, router_group_w, router_group_b, router_expert_w, router_expert_b, expert_w_gate, expert_w_up, expert_w_down):
    batch, ctx_seq, d = x_prompt.shape
    dec_batch, lat_seq, _ = x_sample.shape
    depth = w_mod.shape[0]
    in_width = w_in.shape[2]
    conv_width, conv_ch = conv_dw_w.shape[1:]
    ret_heads = ret_decay.shape[2]
    ret_hd = state_retention.shape[-1]
    att_heads = attn_sink.shape[1]
    kv_heads, att_hd = cache_attn_k.shape[3:]
    past_len = cache_attn_k.shape[2]
    group = att_heads // kv_heads
    n_groups = router_group_w.shape[2]
    n_experts = router_expert_w.shape[2]
    d_expert = expert_w_gate.shape[3]
    nctx, nlat = batch * ctx_seq, dec_batch * lat_seq
    nt = nctx + nlat
    ret_w = ret_heads * ret_hd
    att_w = att_heads * att_hd
    kv_w = kv_heads * att_hd
    c_ca, c_cg = 0, conv_ch
    c_rq = 2 * conv_ch
    c_rk, c_rv, c_rg = c_rq + ret_w, c_rq + 2 * ret_w, c_rq + 3 * ret_w
    c_aq = c_rq + 4 * ret_w
    c_ak = c_aq + att_w
    c_av = c_ak + kv_w
    assert c_av + kv_w == in_width
    assert ctx_seq == CONV_TT == RET_TQ and lat_seq % ATT_TQ == 0 and INPROJ_TM == lat_seq
    assert conv_width // 2 <= CONV_HALO and n_experts + n_groups <= LANES
    assert d == 2 * PACK_SUB * LANES

    mod_rows = SUBLANES
    assert 1 + dec_batch <= mod_rows
    cv = jnp.concatenate([c_ctx[None, :], c, jnp.zeros((mod_rows - 1 - dec_batch, d), F32)], axis=0)
    mods = _modulation(cv, w_mod, b_mod).reshape(depth, mod_rows, 1, 6 * d)

    def mod_row(t0):
        return jnp.where(t0 < nctx, 0, (t0 - nctx) // lat_seq + 1)

    lg = jax.nn.log_sigmoid(ret_decay.astype(F32)).reshape(-1)
    sinks = attn_sink.astype(F32).reshape(-1)
    cos_t, sin_t = _rope_tables(lat_seq, att_hd)
    rw = jnp.concatenate([router_expert_w, router_group_w, jnp.zeros((depth, d, LANES - n_experts - n_groups), F32)], axis=2)
    rb = jnp.concatenate([router_expert_b, router_group_b, jnp.zeros((depth, LANES - n_experts - n_groups), F32)], axis=1)
    rb = rb.reshape(depth, 1, LANES)
    kc_all = cache_attn_k.reshape(dec_batch, depth, past_len, kv_w)
    vc_all = cache_attn_v.reshape(dec_batch, depth, past_len, kv_w)
    n_assign = nt * EXPERT_TOPK
    p_rows = n_assign + (n_experts + 1) * MOE_TM
    n_blk = p_rows // MOE_TM
    smem = pl.BlockSpec(memory_space=pltpu.SMEM)
    hbm = pl.BlockSpec(memory_space=pl.ANY)

    x = jnp.concatenate([x_prompt.reshape(nctx, d), x_sample.reshape(nlat, d)], axis=0)
    k_list, v_list, s_list = [], [], []
    for l in range(depth):
        n_j = in_width // INPROJ_TN
        proj = pl.pallas_call(
            _inproj_kernel,
            out_shape=jax.ShapeDtypeStruct((nt, in_width), BF16),
            grid=(nt // INPROJ_TM, n_j),
            in_specs=[
                pl.BlockSpec((INPROJ_TM, d), lambda i, j: (i, 0)),
                pl.BlockSpec((None, None, 1, d), lambda i, j, l=l: (l, mod_row(i * INPROJ_TM), 0, 1)),
                pl.BlockSpec((None, None, 1, d), lambda i, j, l=l: (l, mod_row(i * INPROJ_TM), 0, 0)),
                pl.BlockSpec((None, 1, d), lambda i, j, l=l: (l, 0, 0)),
                pl.BlockSpec((None, d, INPROJ_TN), lambda i, j, l=l: (l, 0, jnp.where(i == 0, j, n_j - 1))),
            ],
            out_specs=pl.BlockSpec((INPROJ_TM, INPROJ_TN), lambda i, j: (i, j)),
            scratch_shapes=[pltpu.VMEM((INPROJ_TM, d), BF16), pltpu.VMEM((n_j, d, INPROJ_TN), BF16)],
            compiler_params=_cparams(("arbitrary", "arbitrary")),
            name="inproj",
        )(x, mods, mods, norm1_g.reshape(depth, 1, d), w_in)

        hb = CONV_TT // CONV_HALO
        n_hb = nt // CONV_HALO
        conv_out = pl.pallas_call(
            functools.partial(_conv_kernel, nctx=nctx, ctx_seq=ctx_seq, lat_seq=lat_seq, width=conv_width),
            out_shape=jax.ShapeDtypeStruct((nt, conv_ch), BF16),
            grid=(nt // CONV_TT,),
            in_specs=[
                pl.BlockSpec((CONV_TT, conv_ch), lambda i: (i, c_ca // conv_ch)),
                pl.BlockSpec((CONV_TT, conv_ch), lambda i: (i, c_cg // conv_ch)),
                pl.BlockSpec((CONV_HALO, conv_ch), lambda i: (jnp.maximum(i * hb - 1, 0), c_ca // conv_ch)),
                pl.BlockSpec((CONV_HALO, conv_ch), lambda i: (jnp.maximum(i * hb - 1, 0), c_cg // conv_ch)),
                pl.BlockSpec((CONV_HALO, conv_ch), lambda i: (jnp.minimum((i + 1) * hb, n_hb - 1), c_ca // conv_ch)),
                pl.BlockSpec((CONV_HALO, conv_ch), lambda i: (jnp.minimum((i + 1) * hb, n_hb - 1), c_cg // conv_ch)),
                pl.BlockSpec((None, conv_width, conv_ch), lambda i, l=l: (l, 0, 0)),
                pl.BlockSpec((None, 1, conv_ch), lambda i, l=l: (l, 0, 0)),
                pl.BlockSpec((None, 1, conv_ch), lambda i, l=l: (l, 0, 0)),
                pl.BlockSpec((None, 1, conv_ch), lambda i, l=l: (l, 0, 0)),
            ],
            out_specs=pl.BlockSpec((CONV_TT, conv_ch), lambda i: (i, 0)),
            scratch_shapes=[pltpu.VMEM((SUBLANES, CONV_TT + 2 * CONV_HALO, conv_ch), F32)],
            compiler_params=_cparams(("arbitrary",)),
            name="conv",
        )(proj, proj, proj, proj, proj, proj, conv_dw_w, conv_dw_b.reshape(depth, 1, conv_ch),
          conv_ln_g.reshape(depth, 1, conv_ch), conv_ln_b.reshape(depth, 1, conv_ch))

        def ret_call(nseq, seq, row0, has_s0, emit_state):
            nq = seq // RET_TQ
            in_specs = [
                smem,
                pl.BlockSpec((RET_TQ, ret_w), lambda b, qi: (row0 // RET_TQ + b * nq + qi, c_rq // ret_w)),
                pl.BlockSpec((seq, ret_w), lambda b, qi: (row0 // seq + b, c_rk // ret_w)),
                pl.BlockSpec((seq, ret_w), lambda b, qi: (row0 // seq + b, c_rv // ret_w)),
                pl.BlockSpec((RET_TQ, ret_w), lambda b, qi: (row0 // RET_TQ + b * nq + qi, c_rg // ret_w)),
            ]
            args = [lg, proj, proj, proj, proj]
            if has_s0:
                in_specs.append(pl.BlockSpec((None, None, 2, ret_heads, ret_hd, ret_hd),
                                             lambda b, qi, l=l: (b, l, 0, 0, 0, 0)))
                args.append(state_retention)
            out_shape = [jax.ShapeDtypeStruct((nseq * seq, ret_w), BF16)]
            out_specs = [pl.BlockSpec((RET_TQ, ret_w), lambda b, qi: (b * nq + qi, 0))]
            if emit_state:
                out_shape.append(jax.ShapeDtypeStruct((nseq, 2, ret_heads, ret_hd, ret_hd), F32))
                out_specs.append(pl.BlockSpec((None, 2, ret_heads, ret_hd, ret_hd), lambda b, qi: (b, 0, 0, 0, 0)))
            return pl.pallas_call(
                functools.partial(_ret_kernel, layer=l, heads=ret_heads, seq=seq, has_s0=has_s0, emit_state=emit_state),
                out_shape=out_shape,
                grid=(nseq, nq),
                in_specs=in_specs,
                out_specs=out_specs,
                compiler_params=_cparams(("arbitrary", "arbitrary")),
                name="retention",
            )(*args)

        ret_ctx, s_l = ret_call(batch, ctx_seq, 0, False, True)
        (ret_lat,) = ret_call(dec_batch, lat_seq, nctx, True, False)
        s_list.append(s_l)

        gq = q_norm_g.reshape(depth, 1, att_hd)
        gk = k_norm_g.reshape(depth, 1, att_hd)
        att_ctx, k_l, v_l = pl.pallas_call(
            functools.partial(_attn_ctx_kernel, layer=l, group=group, scale=att_hd ** -0.5, nkv=kv_heads),
            out_shape=[jax.ShapeDtypeStruct((nctx, att_w), BF16),
                       jax.ShapeDtypeStruct((batch, ctx_seq, kv_w), F32),
                       jax.ShapeDtypeStruct((batch, ctx_seq, kv_w), F32)],
            grid=(batch,),
            in_specs=[
                smem,
                pl.BlockSpec((ctx_seq, att_w), lambda b: (b, c_aq // att_w)),
                pl.BlockSpec((ctx_seq, kv_w), lambda b: (b, c_ak // kv_w)),
                pl.BlockSpec((ctx_seq, kv_w), lambda b: (b, c_av // kv_w)),
                pl.BlockSpec((None, 1, att_hd), lambda b, l=l: (l, 0, 0)),
                pl.BlockSpec((None, 1, att_hd), lambda b, l=l: (l, 0, 0)),
            ],
            out_specs=[
                pl.BlockSpec((ctx_seq, att_w), lambda b: (b, 0)),
                pl.BlockSpec((None, ctx_seq, kv_w), lambda b: (b, 0, 0)),
                pl.BlockSpec((None, ctx_seq, kv_w), lambda b: (b, 0, 0)),
            ],
            compiler_params=_cparams(("arbitrary",)),
            name="attn_ctx",
        )(sinks, proj, proj, proj, gq, gk)
        k_list.append(k_l)
        v_list.append(v_l)

        nq = lat_seq // ATT_TQ
        att_lat = pl.pallas_call(
            functools.partial(_attn_lat_kernel, layer=l, group=group, scale=att_hd ** -0.5, seq=lat_seq, nkv=kv_heads),
            out_shape=jax.ShapeDtypeStruct((nlat, att_w), BF16),
            grid=(dec_batch, nq),
            in_specs=[
                smem,
                pl.BlockSpec((ATT_TQ, att_w), lambda b, qi: (nctx // ATT_TQ + b * nq + qi, c_aq // att_w)),
                pl.BlockSpec((lat_seq, kv_w), lambda b, qi: (nctx // lat_seq + b, c_ak // kv_w)),
                pl.BlockSpec((lat_seq, kv_w), lambda b, qi: (nctx // lat_seq + b, c_av // kv_w)),
                pl.BlockSpec((None, None, past_len, kv_w), lambda b, qi, l=l: (b, l, 0, 0)),
                pl.BlockSpec((None, None, past_len, kv_w), lambda b, qi, l=l: (b, l, 0, 0)),
                pl.BlockSpec((None, 1, att_hd), lambda b, qi, l=l: (l, 0, 0)),
                pl.BlockSpec((None, 1, att_hd), lambda b, qi, l=l: (l, 0, 0)),
                pl.BlockSpec((lat_seq, att_hd), lambda b, qi: (0, 0)),
                pl.BlockSpec((lat_seq, att_hd), lambda b, qi: (0, 0)),
            ],
            out_specs=pl.BlockSpec((ATT_TQ, att_w), lambda b, qi: (b * nq + qi, 0)),
            scratch_shapes=[pltpu.VMEM((lat_seq, kv_w), BF16)],
            compiler_params=_cparams(("arbitrary", "arbitrary")),
            name="attn_lat",
        )(sinks, proj, proj, proj, kc_all, vc_all, gq, gk, cos_t, sin_t)

        ctx_tiles = nctx // OUT_TM
        lat_tiles = nlat // OUT_TM

        def ctx_rows(i):
            return jnp.minimum(i, ctx_tiles - 1)

        def lat_rows(i):
            return jnp.maximum(i - ctx_tiles, 0)

        def tile_mod(chunk):
            return pl.BlockSpec((None, None, 1, d), lambda s, l=l: (l, mod_row(s * OUT_TM), 0, chunk))

        x1, h2, route, cnt = pl.pallas_call(
            functools.partial(_outproj_kernel, n_groups=n_groups, n_experts=n_experts, ctx_tiles=ctx_tiles),
            out_shape=[jax.ShapeDtypeStruct((nt, d), F32), jax.ShapeDtypeStruct((nt * PACK_SUB, LANES), jnp.uint32),
                       jax.ShapeDtypeStruct((nt, LANES), F32), jax.ShapeDtypeStruct((1, LANES), F32)],
            grid=(ctx_tiles + lat_tiles,),
            in_specs=[
                pl.BlockSpec((OUT_TM, conv_ch), lambda s: (s, 0)),
                pl.BlockSpec((OUT_TM, ret_w), lambda s: (ctx_rows(s), 0)),
                pl.BlockSpec((OUT_TM, ret_w), lambda s: (lat_rows(s), 0)),
                pl.BlockSpec((OUT_TM, att_w), lambda s: (ctx_rows(s), 0)),
                pl.BlockSpec((OUT_TM, att_w), lambda s: (lat_rows(s), 0)),
                pl.BlockSpec((OUT_TM, d), lambda s: (s, 0)),
                tile_mod(2),
                tile_mod(4),
                tile_mod(3),
                pl.BlockSpec((None, 1, d), lambda s, l=l: (l, 0, 0)),
                pl.BlockSpec((None, d, d), lambda s, l=l: (l, 0, 0), pipeline_mode=pl.Buffered(1)),
                pl.BlockSpec((None, d, LANES), lambda s, l=l: (l, 0, 0), pipeline_mode=pl.Buffered(1)),
                pl.BlockSpec((None, 1, LANES), lambda s, l=l: (l, 0, 0)),
            ],
            out_specs=[
                pl.BlockSpec((OUT_TM, d), lambda s: (s, 0)),
                pl.BlockSpec((OUT_TM * PACK_SUB, LANES), lambda s: (s, 0)),
                pl.BlockSpec((OUT_TM, LANES), lambda s: (s, 0)),
                pl.BlockSpec((1, LANES), lambda s: (0, 0)),
            ],
            scratch_shapes=[pltpu.VMEM((d, d), BF16), pltpu.VMEM((d, LANES), BF16), pltpu.VMEM((1, LANES), F32)],
            compiler_params=_cparams(("arbitrary",)),
            name="outproj_router",
        )(conv_out, ret_ctx, ret_lat, att_ctx, att_lat, x, mods, mods, mods, norm2_g.reshape(depth, 1, d), w_out, rw, rb)

        eid = route[:, 0:EXPERT_TOPK].astype(I32)
        rank = route[:, 4:4 + EXPERT_TOPK].astype(I32)
        counts = cnt[0, :n_experts].astype(I32)
        padded = ((counts + MOE_TM - 1) // MOE_TM) * MOE_TM
        pends = jnp.cumsum(padded)
        pstarts = pends - padded
        ek = jnp.arange(n_experts, dtype=I32)
        dest = rank + jnp.sum(jnp.where(eid[:, :, None] == ek, pstarts, 0), axis=-1)
        dest0, dest1 = dest[:, 0], dest[:, 1]
        blk_start = jnp.arange(n_blk, dtype=I32) * MOE_TM
        blk_e = jnp.minimum(jnp.sum((pends[None, :] <= blk_start[:, None]).astype(I32), axis=1), n_experts - 1)
        later = (ek[None, :] > ek[:, None]) & (counts[None, :] > 0)
        nxt_e = jnp.min(jnp.where(later, ek[None, :], n_experts), axis=1)
        nxt_e = jnp.where(nxt_e == n_experts, -1, nxt_e).astype(I32)
        nused = (pends[-1:] // MOE_TM).astype(I32)

        ys = pl.pallas_call(
            functools.partial(_expert_kernel, layer=l),
            out_shape=jax.ShapeDtypeStruct((p_rows * PACK_SUB, LANES), jnp.uint32),
            grid_spec=pltpu.PrefetchScalarGridSpec(
                num_scalar_prefetch=5,
                grid=(n_blk,),
                in_specs=[hbm, hbm, hbm, hbm],
                out_specs=pl.BlockSpec((MOE_TM * PACK_SUB, LANES), lambda b, *_: (b, 0)),
                scratch_shapes=[pltpu.SMEM((p_rows,), I32),
                                pltpu.VMEM((2, MOE_TM * PACK_SUB, LANES), jnp.uint32), pltpu.VMEM((MOE_TM, d), BF16),
                                pltpu.VMEM((MOE_TM, d_expert), BF16),
                                pltpu.VMEM((d, d_expert), F32), pltpu.VMEM((d, d_expert), F32),
                                pltpu.VMEM((d_expert, d), F32),
                                pltpu.VMEM((d, d_expert), BF16), pltpu.VMEM((d, d_expert), BF16),
                                pltpu.VMEM((d_expert, d), BF16),
                                pltpu.SemaphoreType.DMA((2,)), pltpu.SemaphoreType.DMA((3,))],
            ),
            compiler_params=_cparams(("arbitrary",)),
            name="experts",
        )(blk_e, dest0, dest1, nxt_e, nused, h2, expert_w_gate, expert_w_up, expert_w_down)

        x = pl.pallas_call(
            _combine_kernel,
            out_shape=jax.ShapeDtypeStruct((nt, d), F32),
            grid_spec=pltpu.PrefetchScalarGridSpec(
                num_scalar_prefetch=2,
                grid=(nt // CMB_TT,),
                in_specs=[
                    hbm,
                    pl.BlockSpec((CMB_TT, d), lambda i, d0, d1: (i, 0)),
                    pl.BlockSpec((CMB_TT, LANES), lambda i, d0, d1: (i, 0)),
                    pl.BlockSpec((None, None, 1, d), lambda i, d0, d1, l=l: (l, mod_row(i * CMB_TT), 0, 5)),
                ],
                out_specs=pl.BlockSpec((CMB_TT, d), lambda i, d0, d1: (i, 0)),
                scratch_shapes=[pltpu.VMEM((2, EXPERT_TOPK, CMB_TT * PACK_SUB, LANES), jnp.uint32),
                                pltpu.SemaphoreType.DMA((2,))],
            ),
            compiler_params=_cparams(("arbitrary",)),
            name="combine",
        )(dest0, dest1, ys, x1, route, mods)

    y_prompt = x[:nctx].reshape(batch, ctx_seq, d)
    y_sample = x[nctx:].reshape(dec_batch, lat_seq, d)
    new_k = jnp.stack(k_list, axis=1).reshape(batch, depth, ctx_seq, kv_heads, att_hd)
    new_v = jnp.stack(v_list, axis=1).reshape(batch, depth, ctx_seq, kv_heads, att_hd)
    new_s = jnp.stack(s_list, axis=1)
    return (y_prompt, y_sample, new_k, new_v, new_s)
```
